```python
import math
import jax, jax.numpy as jnp
from jax import lax
import numpy as np

D_MODEL = 1024
BATCH = 4
SEQ = 4096
DEPTH = 4
DEC_BATCH = 128
DEC_SEQ = 8
PAST_LEN = 2048
PAGE_SIZE = 128

N_MIXERS = 3
N_ATT_LAYERS = (DEPTH + 2) // 3
N_RET_LAYERS = (DEPTH + 1) // 3
N_RWKV_LAYERS = DEPTH // 3
NORM_EPS = 1e-6
D_FF = 2816
ATT_HEAD_DIM = 64
ATT_HEADS = D_MODEL // (2 * ATT_HEAD_DIM)
ATT_WIDTH = ATT_HEADS * 2 * ATT_HEAD_DIM
ATT_SCALE = ATT_HEAD_DIM ** -0.5
ROT_DIM = ATT_HEAD_DIM // 4
ROPE_THETA = 500000.0
Q_BLOCK = 128
RET_HEADS = 4
RET_KEY_DIM = D_MODEL // RET_HEADS
RET_VALUE_DIM = 2 * RET_KEY_DIM
RET_V_WIDTH = RET_HEADS * RET_VALUE_DIM
RET_CHUNK = 128
RET_THETA = 10000.0
RWKV_HEAD_DIM = 64
RWKV_HEADS = D_MODEL // RWKV_HEAD_DIM
DECAY_LORA = 64
AAA_LORA = 64
GATE_LORA = 128
RWKV_GN_EPS = 64e-5

kernel_name = 'hybrid_diffattn_retention_rwkv7_macaron_step'


def rms_norm(x, gain=None, eps=NORM_EPS):
    xf = x.astype(jnp.float32)
    y = xf * lax.rsqrt(jnp.mean(xf * xf, axis=-1, keepdims=True) + eps)
    if gain is not None:
        y = y * gain.astype(jnp.float32)
    return y.astype(x.dtype)


def half_ffn(x, g, w_gate, w_up, w_down):
    h = rms_norm(x, g)
    return 0.5 * ((jax.nn.silu(h @ w_gate) * (h @ w_up)) @ w_down)


def rotary(x, pos, rot_dim, theta):
    half = rot_dim // 2
    inv_freq = theta ** (-jnp.arange(half, dtype=jnp.float32) * 2.0 / rot_dim)
    ang = pos[:, None] * inv_freq[None, :]
    ang = ang.reshape((1, ang.shape[0]) + (1,) * (x.ndim - 3) + (half,))
    cos = jnp.cos(ang).astype(x.dtype)
    sin = jnp.sin(ang).astype(x.dtype)
    x1 = x[..., :half]
    x2 = x[..., half:rot_dim]
    return jnp.concatenate([x1 * cos - x2 * sin, x2 * cos + x1 * sin, x[..., rot_dim:]], axis=-1)


def diff_lambda(lam_params, lam_init):
    lp = lam_params.astype(jnp.float32)
    return jnp.exp(jnp.sum(lp[0] * lp[1])) - jnp.exp(jnp.sum(lp[2] * lp[3])) + lam_init


def diff_attn_project(h, pos, w_q, w_k, w_v, qk_norm):
    b, t, _ = h.shape
    q = (h @ w_q).reshape(b, t, ATT_HEADS, 2, ATT_HEAD_DIM)
    k = (h @ w_k).reshape(b, t, ATT_HEADS, 2, ATT_HEAD_DIM)
    v = (h @ w_v).reshape(b, t, ATT_HEADS, 2 * ATT_HEAD_DIM)
    q = rotary(rms_norm(q, qk_norm[0]), pos, ROT_DIM, ROPE_THETA)
    k = rotary(rms_norm(k, qk_norm[1]), pos, ROT_DIM, ROPE_THETA)
    return q, k, v


def diff_attn_core(q, k, v, q_pos, k_pos, lam):
    s = jnp.einsum('bqhcd,bkhcd->bhcqk', q, k).astype(jnp.float32) * ATT_SCALE
    s = jnp.where(k_pos[None, :] <= q_pos[:, None], s, -jnp.inf)
    p = jax.nn.softmax(s, axis=-1)
    a = p[:, :, 0] - lam * p[:, :, 1]
    return jnp.einsum('bhqk,bkhe->bqhe', a.astype(v.dtype), v)


def diff_attn_prompt(q, k, v, lam):
    b, t = q.shape[:2]
    qb = min(Q_BLOCK, t)
    nb = t // qb
    k_pos = jnp.arange(t)
    q_blocks = jnp.moveaxis(q.reshape(b, nb, qb, ATT_HEADS, 2, ATT_HEAD_DIM), 1, 0)
    starts = jnp.arange(nb) * qb

    def one_block(args):
        q_blk, s0 = args
        return diff_attn_core(q_blk, k, v, s0 + jnp.arange(qb), k_pos, lam)

    o = lax.map(one_block, (q_blocks, starts))
    return jnp.moveaxis(o, 0, 1).reshape(b, t, ATT_HEADS, 2 * ATT_HEAD_DIM)


def diff_attn_out(o, subln, lam_init, w_o):
    b, t = o.shape[:2]
    o = rms_norm(o, subln) * (1.0 - lam_init)
    return o.reshape(b, t, ATT_WIDTH) @ w_o


def retention_chunked(q, k, v, s0):
    b, l = q.shape[:2]
    c = min(RET_CHUNK, l)
    n = l // c
    lg = jnp.log(1.0 - 2.0 ** (-5.0 - jnp.arange(RET_HEADS, dtype=jnp.float32)))
    idx = jnp.arange(c, dtype=jnp.float32)
    diff = idx[:, None] - idx[None, :]
    d_intra = jnp.where(diff[None] >= 0.0, jnp.exp(jnp.maximum(diff, 0.0)[None] * lg[:, None, None]), 0.0)
    q_dec = jnp.exp((idx[:, None] + 1.0) * lg[None, :])
    k_dec = jnp.exp((c - 1.0 - idx)[:, None] * lg[None, :])
    c_dec = jnp.exp(c * lg)

    def chunk(s, xs):
        qc, kc, vc = xs
        att = jnp.einsum('bihd,bjhd->bhij', qc, kc) * d_intra
        o = (jnp.einsum('bhij,bjhe->bihe', att, vc)
             + jnp.einsum('bihd,bhde->bihe', qc * q_dec[None, :, :, None], s))
        s = s * c_dec[None, :, None, None] + jnp.einsum('bjhd,bjhe->bhde', kc * k_dec[None, :, :, None], vc)
        return s, o

    def to_chunks(a):
        return jnp.moveaxis(a.reshape((b, n, c) + a.shape[2:]), 1, 0)

    s, o = lax.scan(chunk, s0, (to_chunks(q), to_chunks(k), to_chunks(v)))
    return jnp.moveaxis(o, 0, 1).reshape((b, l) + o.shape[3:]), s


def retention_mixer(h, pos, s0, w_q, w_k, w_v, w_g, w_o):
    b, t, _ = h.shape
    q = (h @ w_q).reshape(b, t, RET_HEADS, RET_KEY_DIM)
    k = (h @ w_k).reshape(b, t, RET_HEADS, RET_KEY_DIM) * (RET_KEY_DIM ** -0.5)
    v = (h @ w_v).reshape(b, t, RET_HEADS, RET_VALUE_DIM)
    q = rotary(q, pos, RET_KEY_DIM, RET_THETA)
    k = rotary(k, pos, RET_KEY_DIM, RET_THETA)
    o, s = retention_chunked(q.astype(jnp.float32), k.astype(jnp.float32),
                             v.astype(jnp.float32), s0.astype(jnp.float32))
    o = rms_norm(o).astype(h.dtype).reshape(b, t, RET_V_WIDTH)
    return (jax.nn.silu(h @ w_g) * o) @ w_o, s.astype(s0.dtype)


def rwkv7_scan(r, w, k, v, a, bb, s0):
    def step(s, xs):
        rt, wt, kt, vt, at, bt = xs
        sa = jnp.einsum('bhij,bhj->bhi', s, at)
        s = s * wt[:, :, None, :] + sa[..., None] * bt[:, :, None, :] + vt[..., None] * kt[:, :, None, :]
        return s, jnp.einsum('bhij,bhj->bhi', s, rt)

    xs = tuple(jnp.moveaxis(z, 1, 0) for z in (r, w, k, v, a, bb))
    s, y = lax.scan(step, s0, xs)
    return jnp.moveaxis(y, 0, 1), s


def rwkv7_mixer(h, shift_prev, s0, mix, w_r, w_k, w_v, w_o, w0, w1, w2, a0, a1, a2,
                g1, g2, k_k, k_a, r_k, ln_w, ln_b):
    b, t, _ = h.shape
    xx = jnp.concatenate([shift_prev[:, None, :], h[:, :-1]], axis=1) - h
    xr, xw, xk, xv, xa, xg = [h + xx * mix[m] for m in range(6)]
    r = xr @ w_r
    w_log = -jax.nn.softplus(-(w0 + jnp.tanh(xw @ w1) @ w2)) - 0.5
    decay = jnp.exp(-jnp.exp(w_log.astype(jnp.float32)))
    k = xk @ w_k
    v = xv @ w_v
    a = jax.nn.sigmoid(a0 + (xa @ a1) @ a2)
    g = jax.nn.sigmoid(xg @ g1) @ g2

    def heads(z):
        return z.reshape(b, t, RWKV_HEADS, RWKV_HEAD_DIM).astype(jnp.float32)

    kk = heads(k * k_k)
    kk = kk / jnp.maximum(jnp.sqrt(jnp.sum(kk * kk, axis=-1, keepdims=True)), 1e-12)
    k = k * (1.0 + (a - 1.0) * k_a)
    rh, kh, vh, ah, wh = heads(r), heads(k), heads(v), heads(a), heads(decay)
    y, s = rwkv7_scan(rh, wh, kh, vh, -kk, kk * ah, s0.astype(jnp.float32))
    mu = jnp.mean(y, axis=-1, keepdims=True)
    var = jnp.mean(jnp.square(y - mu), axis=-1, keepdims=True)
    y = ((y - mu) * lax.rsqrt(var + RWKV_GN_EPS)).reshape(b, t, D_MODEL)
    y = y * ln_w.astype(jnp.float32) + ln_b.astype(jnp.float32)
    bonus = (jnp.sum(rh * kh * r_k.astype(jnp.float32), axis=-1, keepdims=True) * vh).reshape(b, t, D_MODEL)
    out = ((y + bonus).astype(h.dtype) * g) @ w_o
    return out, s.astype(s0.dtype), h[:, -1]


def setup_inputs(seed: int = 0) -> dict:
    key = jax.random.key(seed)
    keys = iter(jax.random.split(key, 64))

    def nrm(shape, scale):
        return jax.random.normal(next(keys), shape, jnp.float32) * scale

    d = D_MODEL
    n_pages = PAST_LEN // PAGE_SIZE
    n_used = DEC_BATCH * n_pages
    n_pool = n_used + n_used // 4
    return {
        'x_prompt': nrm((BATCH, SEQ, d), 1.0),
        'x_sample': nrm((DEC_BATCH, DEC_SEQ, d), 1.0),
        'cache_k': nrm((N_ATT_LAYERS, n_pool, PAGE_SIZE, ATT_HEADS, 2 * ATT_HEAD_DIM), 1.0),
        'cache_v': nrm((N_ATT_LAYERS, n_pool, PAGE_SIZE, ATT_HEADS, 2 * ATT_HEAD_DIM), 1.0),
        'page_table': jax.random.permutation(next(keys), n_pool)[:n_used].reshape(DEC_BATCH, n_pages).astype(jnp.int32),
        'state_ret': nrm((N_RET_LAYERS, DEC_BATCH, RET_HEADS, RET_KEY_DIM, RET_VALUE_DIM), 0.5),
        'state_rwkv': nrm((N_RWKV_LAYERS, DEC_BATCH, RWKV_HEADS, RWKV_HEAD_DIM, RWKV_HEAD_DIM), 0.5),
        'state_rwkv_shift': nrm((N_RWKV_LAYERS, DEC_BATCH, d), 1.0),
        'norm_g': 1.0 + nrm((DEPTH, 3, d), 0.05),
        'ffn_w_gate': nrm((DEPTH, 2, d, D_FF), d ** -0.5),
        'ffn_w_up': nrm((DEPTH, 2, d, D_FF), d ** -0.5),
        'ffn_w_down': nrm((DEPTH, 2, D_FF, d), D_FF ** -0.5),
        'attn_w_q': nrm((N_ATT_LAYERS, d, ATT_WIDTH), d ** -0.5),
        'attn_w_k': nrm((N_ATT_LAYERS, d, ATT_WIDTH), d ** -0.5),
        'attn_w_v': nrm((N_ATT_LAYERS, d, ATT_WIDTH), d ** -0.5),
        'attn_w_o': nrm((N_ATT_LAYERS, ATT_WIDTH, d), ATT_WIDTH ** -0.5),
        'attn_qk_norm': 1.0 + nrm((N_ATT_LAYERS, 2, ATT_HEAD_DIM), 0.05),
        'attn_lambda': nrm((N_ATT_LAYERS, 4, ATT_HEAD_DIM), 0.1),
        'attn_subln': 1.0 + nrm((N_ATT_LAYERS, 2 * ATT_HEAD_DIM), 0.05),
        'ret_w_q': nrm((N_RET_LAYERS, d, RET_HEADS * RET_KEY_DIM), d ** -0.5),
        'ret_w_k': nrm((N_RET_LAYERS, d, RET_HEADS * RET_KEY_DIM), d ** -0.5),
        'ret_w_v': nrm((N_RET_LAYERS, d, RET_V_WIDTH), d ** -0.5),
        'ret_w_g': nrm((N_RET_LAYERS, d, RET_V_WIDTH), d ** -0.5),
        'ret_w_o': nrm((N_RET_LAYERS, RET_V_WIDTH, d), RET_V_WIDTH ** -0.5),
        'rwkv_mix': jax.random.uniform(next(keys), (N_RWKV_LAYERS, 6, d), jnp.float32),
        'rwkv_w_r': nrm((N_RWKV_LAYERS, d, d), d ** -0.5),
        'rwkv_w_k': nrm((N_RWKV_LAYERS, d, d), d ** -0.5),
        'rwkv_w_v': nrm((N_RWKV_LAYERS, d, d), d ** -0.5),
        'rwkv_w_o': nrm((N_RWKV_LAYERS, d, d), d ** -0.5),
        'rwkv_w0': nrm((N_RWKV_LAYERS, d), 1.0),
        'rwkv_w1': nrm((N_RWKV_LAYERS, d, DECAY_LORA), d ** -0.5),
        'rwkv_w2': nrm((N_RWKV_LAYERS, DECAY_LORA, d), 0.5 * DECAY_LORA ** -0.5),
        'rwkv_a0': nrm((N_RWKV_LAYERS, d), 0.5),
        'rwkv_a1': nrm((N_RWKV_LAYERS, d, AAA_LORA), d ** -0.5),
        'rwkv_a2': nrm((N_RWKV_LAYERS, AAA_LORA, d), 0.5 * AAA_LORA ** -0.5),
        'rwkv_g1': nrm((N_RWKV_LAYERS, d, GATE_LORA), d ** -0.5),
        'rwkv_g2': nrm((N_RWKV_LAYERS, GATE_LORA, d), GATE_LORA ** -0.5),
        'rwkv_k_k': 0.85 + nrm((N_RWKV_LAYERS, d), 0.05),
        'rwkv_k_a': 1.0 + nrm((N_RWKV_LAYERS, d), 0.05),
        'rwkv_r_k': nrm((N_RWKV_LAYERS, RWKV_HEADS, RWKV_HEAD_DIM), 0.1),
        'rwkv_ln_w': 1.0 + nrm((N_RWKV_LAYERS, d), 0.05),
        'rwkv_ln_b': nrm((N_RWKV_LAYERS, d), 0.02),
    }


def reference(x_prompt, x_sample, cache_k, cache_v, page_table, state_ret, state_rwkv, state_rwkv_shift,
              norm_g, ffn_w_gate, ffn_w_up, ffn_w_down,
              attn_w_q, attn_w_k, attn_w_v, attn_w_o, attn_qk_norm, attn_lambda, attn_subln,
              ret_w_q, ret_w_k, ret_w_v, ret_w_g, ret_w_o,
              rwkv_mix, rwkv_w_r, rwkv_w_k, rwkv_w_v, rwkv_w_o, rwkv_w0, rwkv_w1, rwkv_w2,
              rwkv_a0, rwkv_a1, rwkv_a2, rwkv_g1, rwkv_g2, rwkv_k_k, rwkv_k_a, rwkv_r_k,
              rwkv_ln_w, rwkv_ln_b):
    b_p, t_p = x_prompt.shape[:2]
    b_s, t_s = x_sample.shape[:2]
    past = page_table.shape[1] * cache_k.shape[2]
    pos_p = jnp.arange(t_p, dtype=jnp.float32)
    pos_s = past + jnp.arange(t_s, dtype=jnp.float32)
    yp, ys = x_prompt, x_sample
    akp, avp, aks, avs = [], [], [], []
    rsp, rss, wsp, wss, shp, shs = [], [], [], [], [], []
    for i in range(DEPTH):
        kind, j = i % N_MIXERS, i // N_MIXERS
        yp = yp + half_ffn(yp, norm_g[i, 0], ffn_w_gate[i, 0], ffn_w_up[i, 0], ffn_w_down[i, 0])
        ys = ys + half_ffn(ys, norm_g[i, 0], ffn_w_gate[i, 0], ffn_w_up[i, 0], ffn_w_down[i, 0])
        hp = rms_norm(yp, norm_g[i, 1])
        hs = rms_norm(ys, norm_g[i, 1])
        if kind == 0:
            lam_init = 0.8 - 0.6 * math.exp(-0.3 * i)
            lam = diff_lambda(attn_lambda[j], lam_init)
            qp, kp, vp = diff_attn_project(hp, pos_p, attn_w_q[j], attn_w_k[j], attn_w_v[j], attn_qk_norm[j])
            qs, ks, vs = diff_attn_project(hs, pos_s, attn_w_q[j], attn_w_k[j], attn_w_v[j], attn_qk_norm[j])
            op = diff_attn_prompt(qp, kp, vp, lam)
            pk = cache_k[j][page_table].reshape(b_s, past, ATT_HEADS, 2, ATT_HEAD_DIM)
            pv = cache_v[j][page_table].reshape(b_s, past, ATT_HEADS, 2 * ATT_HEAD_DIM)
            o_s = diff_attn_core(qs, jnp.concatenate([pk, ks], axis=1), jnp.concatenate([pv, vs], axis=1),
                                 past + jnp.arange(t_s), jnp.arange(past + t_s), lam)
            mp = diff_attn_out(op, attn_subln[j], lam_init, attn_w_o[j])
            ms = diff_attn_out(o_s, attn_subln[j], lam_init, attn_w_o[j])
            akp.append(kp.reshape(b_p, t_p, ATT_HEADS, 2 * ATT_HEAD_DIM))
            avp.append(vp)
            aks.append(ks.reshape(b_s, t_s, ATT_HEADS, 2 * ATT_HEAD_DIM))
            avs.append(vs)
        elif kind == 1:
            s0_p = jnp.zeros((b_p, RET_HEADS, RET_KEY_DIM, RET_VALUE_DIM), yp.dtype)
            mp, sp = retention_mixer(hp, pos_p, s0_p, ret_w_q[j], ret_w_k[j], ret_w_v[j], ret_w_g[j], ret_w_o[j])
            ms, ss = retention_mixer(hs, pos_s, state_ret[j], ret_w_q[j], ret_w_k[j], ret_w_v[j], ret_w_g[j], ret_w_o[j])
            rsp.append(sp)
            rss.append(ss)
        else:
            rw = (rwkv_mix[j], rwkv_w_r[j], rwkv_w_k[j], rwkv_w_v[j], rwkv_w_o[j], rwkv_w0[j], rwkv_w1[j],
                  rwkv_w2[j], rwkv_a0[j], rwkv_a1[j], rwkv_a2[j], rwkv_g1[j], rwkv_g2[j], rwkv_k_k[j],
                  rwkv_k_a[j], rwkv_r_k[j], rwkv_ln_w[j], rwkv_ln_b[j])
            sh0_p = jnp.zeros((b_p, D_MODEL), yp.dtype)
            s0_p = jnp.zeros((b_p, RWKV_HEADS, RWKV_HEAD_DIM, RWKV_HEAD_DIM), yp.dtype)
            mp, sp, lp = rwkv7_mixer(hp, sh0_p, s0_p, *rw)
            ms, ss, ls = rwkv7_mixer(hs, state_rwkv_shift[j], state_rwkv[j], *rw)
            wsp.append(sp)
            wss.append(ss)
            shp.append(lp)
            shs.append(ls)
        yp = yp + mp
        ys = ys + ms
        yp = yp + half_ffn(yp, norm_g[i, 2], ffn_w_gate[i, 1], ffn_w_up[i, 1], ffn_w_down[i, 1])
        ys = ys + half_ffn(ys, norm_g[i, 2], ffn_w_gate[i, 1], ffn_w_up[i, 1], ffn_w_down[i, 1])
    return (yp, ys,
            jnp.stack(akp), jnp.stack(avp), jnp.stack(aks), jnp.stack(avs),
            jnp.stack(rsp), jnp.stack(rss), jnp.stack(wsp), jnp.stack(wss),
            jnp.stack(shp), jnp.stack(shs))
```

```python
import functools
import math

import jax
import jax.numpy as jnp
from jax import lax
from jax.experimental import pallas as pl
from jax.experimental.pallas import tpu as pltpu

F32 = jnp.float32
BF16 = jnp.bfloat16

D_MODEL = 1024
D_FF = 2816
NORM_EPS = 1e-6
ATT_HEAD_DIM = 64
ATT_HEADS = 8
ATT_SCALE = ATT_HEAD_DIM ** -0.5
ROT_DIM = 16
ROPE_THETA = 500000.0
RET_HEADS = 4
RET_KEY_DIM = 256
RET_VALUE_DIM = 512
RET_CHUNK = 128
RET_THETA = 10000.0
RWKV_HEAD_DIM = 64
RWKV_HEADS = 16
RWKV_GN_EPS = 64e-5
RWKV_CHUNK = 64

LANES = 128
MXU_DIM = 256
VMEM_LIMIT = 56 * 1024 * 1024
TOKEN_TILE = 512
FF_CHUNK = 256


def _params(sem):
    return pltpu.CompilerParams(dimension_semantics=sem, vmem_limit_bytes=VMEM_LIMIT)


def _dot(a, b):
    return jnp.dot(a.astype(BF16), b.astype(BF16), preferred_element_type=F32)


def _dot_nt(a, b):
    return lax.dot_general(a.astype(BF16), b.astype(BF16), (((1,), (1,)), ((), ())),
                           preferred_element_type=F32)


def _rms(x, gain=None, eps=NORM_EPS):
    y = x * lax.rsqrt(jnp.mean(x * x, axis=-1, keepdims=True) + eps)
    return y if gain is None else y * gain


def _const_spec(shape):
    nd = len(shape)
    return pl.BlockSpec(shape, lambda *_: (0,) * nd)


def _row_spec(tm, width):
    return pl.BlockSpec((tm, width), lambda i: (i, 0))


def _tile(n):
    tm = min(TOKEN_TILE, n)
    assert n % tm == 0
    return tm


def _block_diag_ones(group):
    idx = jnp.arange(MXU_DIM) // group
    return (idx[:, None] == idx[None, :]).astype(BF16)


def _group_sum(x, bd_ref, split=False):
    parts = []
    for c in range(x.shape[1] // MXU_DIM):
        xc = x[:, c * MXU_DIM:(c + 1) * MXU_DIM]
        hi = xc.astype(BF16)
        s = jnp.dot(hi, bd_ref[...], preferred_element_type=F32)
        if split:
            lo = (xc - hi.astype(F32)).astype(BF16)
            s = s + jnp.dot(lo, bd_ref[...], preferred_element_type=F32)
        parts.append(s)
    return jnp.concatenate(parts, axis=1)


def _ffn_body(x_ref, g_ref, wg_ref, wu_ref, wd_ref, o_ref):
    x = x_ref[...]
    h = _rms(x, g_ref[...]).astype(BF16)
    acc = jnp.zeros(x.shape, F32)
    for c in range(D_FF // FF_CHUNK):
        sl = slice(c * FF_CHUNK, (c + 1) * FF_CHUNK)
        gate = jnp.dot(h, wg_ref[:, sl], preferred_element_type=F32)
        up = jnp.dot(h, wu_ref[:, sl], preferred_element_type=F32)
        act = (gate * jax.nn.sigmoid(gate) * up).astype(BF16)
        acc = acc + jnp.dot(act, wd_ref[sl, :], preferred_element_type=F32)
    o_ref[...] = x + 0.5 * acc


def _ffn(x, g, wg, wu, wd):
    n = x.shape[0]
    tm = _tile(n)
    return pl.pallas_call(
        _ffn_body,
        grid=(n // tm,),
        in_specs=[_row_spec(tm, D_MODEL), _const_spec((1, D_MODEL)),
                  _const_spec((D_MODEL, D_FF)), _const_spec((D_MODEL, D_FF)),
                  _const_spec((D_FF, D_MODEL))],
        out_specs=_row_spec(tm, D_MODEL),
        out_shape=jax.ShapeDtypeStruct((n, D_MODEL), F32),
        compiler_params=_params(("parallel",)),
        name="ffn",
    )(x, g.reshape(1, D_MODEL), wg, wu, wd)


def _out_proj_body(x_ref, a_ref, w_ref, o_ref):
    o_ref[...] = x_ref[...] + jnp.dot(a_ref[...].astype(BF16), w_ref[...], preferred_element_type=F32)


def _out_proj(x, a, w):
    n, k = a.shape
    tm = _tile(n)
    return pl.pallas_call(
        _out_proj_body,
        grid=(n // tm,),
        in_specs=[_row_spec(tm, D_MODEL), _row_spec(tm, k), _const_spec((k, D_MODEL))],
        out_specs=_row_spec(tm, D_MODEL),
        out_shape=jax.ShapeDtypeStruct((n, D_MODEL), F32),
        compiler_params=_params(("parallel",)),
        name="out_proj",
    )(x, a, w)


def _attn_rot_tables(pos):
    half = ROT_DIM // 2
    inv_freq = ROPE_THETA ** (-jnp.arange(half, dtype=jnp.float32) * 2.0 / ROT_DIM)
    ang = pos[:, None] * inv_freq[None, :]
    cos, sin = jnp.cos(ang), jnp.sin(ang)
    n = pos.shape[0]
    ones = jnp.ones((n, ATT_HEAD_DIM - ROT_DIM), jnp.float32)
    zeros = jnp.zeros((n, ATT_HEAD_DIM - ROT_DIM), jnp.float32)
    zh = jnp.zeros((n, half), jnp.float32)
    c = jnp.concatenate([cos, cos, ones], axis=1)
    s_up = jnp.concatenate([-sin, zh, zeros], axis=1)
    s_dn = jnp.concatenate([zh, sin, zeros], axis=1)
    rep = LANES // ATT_HEAD_DIM
    return tuple(jnp.tile(t, (1, rep)) for t in (c, s_up, s_dn))


def _attn_proj_body(x_ref, g_ref, wq_ref, wk_ref, wv_ref, qkn_ref, bd_ref, c_ref, su_ref, sd_ref,
                    q_ref, k_ref, kb_ref, v_ref, vb_ref):
    h = _rms(x_ref[...], g_ref[...]).astype(BF16)
    rep = D_MODEL // LANES
    cos = jnp.concatenate([c_ref[...]] * rep, axis=1)
    s_up = jnp.concatenate([su_ref[...]] * rep, axis=1)
    s_dn = jnp.concatenate([sd_ref[...]] * rep, axis=1)
    half = ROT_DIM // 2

    def norm_rot(z, gain):
        ms = _group_sum(z * z, bd_ref) * (1.0 / ATT_HEAD_DIM)
        z = z * lax.rsqrt(ms + NORM_EPS) * gain
        up = pltpu.roll(z, D_MODEL - half, 1)
        dn = pltpu.roll(z, half, 1)
        return z * cos + up * s_up + dn * s_dn

    q = norm_rot(jnp.dot(h, wq_ref[...], preferred_element_type=F32), qkn_ref[0:1, :])
    q_ref[...] = (q * ATT_SCALE).astype(BF16)
    k = norm_rot(jnp.dot(h, wk_ref[...], preferred_element_type=F32), qkn_ref[1:2, :])
    k_ref[...] = k
    kb_ref[...] = k.astype(BF16)
    v = jnp.dot(h, wv_ref[...], preferred_element_type=F32)
    v_ref[...] = v
    vb_ref[...] = v.astype(BF16)


def _attn_proj(x, g, wq, wk, wv, qk_norm, pos):
    n = x.shape[0]
    tm = _tile(n)
    qkn = jnp.tile(qk_norm, (1, D_MODEL // ATT_HEAD_DIM))
    tabs = _attn_rot_tables(pos)
    w_spec = _const_spec((D_MODEL, D_MODEL))
    row = _row_spec(tm, D_MODEL)
    tab = _row_spec(tm, LANES)
    f = jax.ShapeDtypeStruct((n, D_MODEL), F32)
    b = jax.ShapeDtypeStruct((n, D_MODEL), BF16)
    return pl.pallas_call(
        _attn_proj_body,
        grid=(n // tm,),
        in_specs=[row, _const_spec((1, D_MODEL)), w_spec, w_spec, w_spec, _const_spec((2, D_MODEL)),
                  _const_spec((MXU_DIM, MXU_DIM)), tab, tab, tab],
        out_specs=[row] * 5,
        out_shape=[b, f, b, f, b],
        compiler_params=_params(("parallel",)),
        name="attn_proj",
    )(x, g.reshape(1, D_MODEL), wq, wk, wv, qkn, _block_diag_ones(ATT_HEAD_DIM), *tabs)


def _half_masks(rows):
    lane = lax.broadcasted_iota(jnp.int32, (rows, LANES), 1)
    return lane < ATT_HEAD_DIM


def _stack_halves(q):
    first = _half_masks(q.shape[0])
    zero = jnp.zeros_like(q)
    return jnp.concatenate([jnp.where(first, q, zero), jnp.where(first, zero, q)], axis=0)


def _softmax_step(s, v, m_ref, l_ref, acc_ref):
    m_prev = m_ref[...]
    m_new = jnp.maximum(m_prev, jnp.max(s, axis=-1, keepdims=True))
    alpha = jnp.exp(m_prev - m_new)
    p = jnp.exp(s - m_new[:, 0:1])
    l_ref[...] = alpha * l_ref[...] + jnp.sum(p, axis=-1, keepdims=True)
    acc_ref[...] = alpha * acc_ref[...] + jnp.dot(p.astype(BF16), v, preferred_element_type=F32)
    m_ref[...] = m_new


def _diff_combine(t, lam_ref, subln_ref, lam_init, l_ref, acc_ref):
    lp = lam_ref[...]
    lam = (jnp.exp(jnp.sum(lp[0:1] * lp[1:2], axis=-1, keepdims=True))
           - jnp.exp(jnp.sum(lp[2:3] * lp[3:4], axis=-1, keepdims=True)) + lam_init)
    o = acc_ref[0:t, :] / l_ref[0:t, :] - lam * (acc_ref[t:2 * t, :] / l_ref[t:2 * t, :])
    return _rms(o, subln_ref[...]) * (1.0 - lam_init)


def _attn_prompt_body(q_ref, k_ref, v_ref, lam_ref, subln_ref, o_ref, m_ref, l_ref, acc_ref, *, tq, lam_init):
    i = pl.program_id(2)
    q2 = _stack_halves(q_ref[...])
    m_ref[...] = jnp.full(m_ref.shape, -jnp.inf, F32)
    l_ref[...] = jnp.zeros(l_ref.shape, F32)
    acc_ref[...] = jnp.zeros(acc_ref.shape, F32)

    def block(j, carry):
        off = pl.multiple_of(j * tq, tq)
        s = _dot_nt(q2, k_ref[pl.ds(off, tq), :])
        _softmax_step(s, v_ref[pl.ds(off, tq), :], m_ref, l_ref, acc_ref)
        return carry

    lax.fori_loop(0, i, block, 0)
    off = pl.multiple_of(i * tq, tq)
    s = _dot_nt(q2, k_ref[pl.ds(off, tq), :])
    row = lax.broadcasted_iota(jnp.int32, (2 * tq, tq), 0)
    col = lax.broadcasted_iota(jnp.int32, (2 * tq, tq), 1)
    row = jnp.where(row >= tq, row - tq, row)
    s = jnp.where(col <= row, s, -jnp.inf)
    _softmax_step(s, v_ref[pl.ds(off, tq), :], m_ref, l_ref, acc_ref)
    o_ref[...] = _diff_combine(tq, lam_ref, subln_ref, lam_init, l_ref, acc_ref)


def _attn_prompt(q, k, v, lam_params, subln, lam_init, batch, seq):
    tq = min(256, seq)
    nq = seq // tq
    body = functools.partial(_attn_prompt_body, tq=tq, lam_init=lam_init)
    kv_spec = pl.BlockSpec((seq, LANES), lambda b, h, i: (b, h))
    return pl.pallas_call(
        body,
        grid=(batch, ATT_HEADS, nq),
        in_specs=[pl.BlockSpec((tq, LANES), lambda b, h, i: (b * nq + i, h)), kv_spec, kv_spec,
                  _const_spec((4, ATT_HEAD_DIM)), _const_spec((1, LANES))],
        out_specs=pl.BlockSpec((tq, LANES), lambda b, h, i: (b * nq + i, h)),
        out_shape=jax.ShapeDtypeStruct((batch * seq, D_MODEL), F32),
        scratch_shapes=[pltpu.VMEM((2 * tq, LANES), F32)] * 3,
        compiler_params=_params(("parallel", "parallel", "parallel")),
        name="attn_prompt",
    )(q, k, v, lam_params, subln.reshape(1, LANES))


def _attn_sample_body(pt_ref, q_ref, kn_ref, vn_ref, ck_ref, cv_ref, lam_ref, subln_ref, o_ref,
                      m_ref, l_ref, acc_ref, *, t, page, lam_init):
    del pt_ref
    p = pl.program_id(1)

    @pl.when(p == 0)
    def _():
        m_ref[...] = jnp.full(m_ref.shape, -jnp.inf, F32)
        l_ref[...] = jnp.zeros(l_ref.shape, F32)
        acc_ref[...] = jnp.zeros(acc_ref.shape, F32)

    for h in range(ATT_HEADS):
        sl = slice(h * LANES, (h + 1) * LANES)
        q2 = _stack_halves(q_ref[:, sl])
        kh = ck_ref[pl.ds(h, page, stride=ATT_HEADS), :]
        vh = cv_ref[pl.ds(h, page, stride=ATT_HEADS), :].astype(BF16)
        _softmax_step(_dot_nt(q2, kh), vh, m_ref.at[h], l_ref.at[h], acc_ref.at[h])

    @pl.when(p == pl.num_programs(1) - 1)
    def _():
        row = lax.broadcasted_iota(jnp.int32, (2 * t, t), 0)
        col = lax.broadcasted_iota(jnp.int32, (2 * t, t), 1)
        row = jnp.where(row >= t, row - t, row)
        for h in range(ATT_HEADS):
            sl = slice(h * LANES, (h + 1) * LANES)
            q2 = _stack_halves(q_ref[:, sl])
            s = jnp.where(col <= row, _dot_nt(q2, kn_ref[:, sl]), -jnp.inf)
            _softmax_step(s, vn_ref[:, sl], m_ref.at[h], l_ref.at[h], acc_ref.at[h])
            o_ref[:, sl] = _diff_combine(t, lam_ref, subln_ref, lam_init, l_ref.at[h], acc_ref.at[h])


def _attn_sample(q, kn, vn, cache_k, cache_v, page_table, layer, lam_params, subln, lam_init, t):
    batch, n_pages = page_table.shape
    page = cache_k.shape[2]
    rows = page * ATT_HEADS
    ck = cache_k.reshape(cache_k.shape[:2] + (rows, LANES))
    cv = cache_v.reshape(cache_v.shape[:2] + (rows, LANES))
    q, kn, vn = (z.reshape(batch, t, D_MODEL) for z in (q, kn, vn))
    body = functools.partial(_attn_sample_body, t=t, page=page, lam_init=lam_init)
    tok = pl.BlockSpec((None, t, D_MODEL), lambda b, p, pt: (b, 0, 0))
    cache = pl.BlockSpec((None, None, rows, LANES), lambda b, p, pt: (layer, pt[b, p], 0, 0))
    grid_spec = pltpu.PrefetchScalarGridSpec(
        num_scalar_prefetch=1,
        grid=(batch, n_pages),
        in_specs=[tok, tok, tok, cache, cache,
                  pl.BlockSpec((4, ATT_HEAD_DIM), lambda b, p, pt: (0, 0)),
                  pl.BlockSpec((1, LANES), lambda b, p, pt: (0, 0))],
        out_specs=tok,
        scratch_shapes=[pltpu.VMEM((ATT_HEADS, 2 * t, LANES), F32)] * 3,
    )
    return pl.pallas_call(
        body,
        grid_spec=grid_spec,
        out_shape=jax.ShapeDtypeStruct((batch, t, D_MODEL), F32),
        compiler_params=_params(("parallel", "arbitrary")),
        name="attn_sample",
    )(page_table, q, kn, vn, ck, cv, lam_params, subln.reshape(1, LANES)).reshape(batch * t, D_MODEL)


def _ret_rot_tables(pos):
    half = RET_KEY_DIM // 2
    inv_freq = RET_THETA ** (-jnp.arange(half, dtype=jnp.float32) * 2.0 / RET_KEY_DIM)
    ang = pos[:, None] * inv_freq[None, :]
    return jnp.cos(ang), jnp.sin(ang)


def _ret_proj_body(x_ref, g_ref, wq_ref, wk_ref, wv_ref, wg_ref, c_ref, s_ref, q_ref, k_ref, v_ref, sg_ref):
    h = _rms(x_ref[...], g_ref[...]).astype(BF16)
    cos, sin = c_ref[...], s_ref[...]

    def rot(z):
        parts = []
        for hd in range(RET_HEADS):
            x1 = z[:, hd * RET_KEY_DIM:hd * RET_KEY_DIM + LANES]
            x2 = z[:, hd * RET_KEY_DIM + LANES:(hd + 1) * RET_KEY_DIM]
            parts += [x1 * cos - x2 * sin, x2 * cos + x1 * sin]
        return jnp.concatenate(parts, axis=1)

    q_ref[...] = rot(jnp.dot(h, wq_ref[...], preferred_element_type=F32))
    k_ref[...] = rot(jnp.dot(h, wk_ref[...], preferred_element_type=F32) * (RET_KEY_DIM ** -0.5))
    v_ref[...] = jnp.dot(h, wv_ref[...], preferred_element_type=F32)
    gate = jnp.dot(h, wg_ref[...], preferred_element_type=F32)
    sg_ref[...] = gate * jax.nn.sigmoid(gate)


def _ret_proj(x, g, wq, wk, wv, wg, pos):
    n = x.shape[0]
    tm = _tile(n)
    cos, sin = _ret_rot_tables(pos)
    vw = RET_HEADS * RET_VALUE_DIM
    row = _row_spec(tm, D_MODEL)
    wide = _row_spec(tm, vw)
    tab = _row_spec(tm, LANES)
    return pl.pallas_call(
        _ret_proj_body,
        grid=(n // tm,),
        in_specs=[row, _const_spec((1, D_MODEL)), _const_spec((D_MODEL, D_MODEL)), _const_spec((D_MODEL, D_MODEL)),
                  _const_spec((D_MODEL, vw)), _const_spec((D_MODEL, vw)), tab, tab],
        out_specs=[row, row, wide, wide],
        out_shape=[jax.ShapeDtypeStruct((n, D_MODEL), F32), jax.ShapeDtypeStruct((n, D_MODEL), F32),
                   jax.ShapeDtypeStruct((n, vw), F32), jax.ShapeDtypeStruct((n, vw), F32)],
        compiler_params=_params(("parallel",)),
        name="ret_proj",
    )(x, g.reshape(1, D_MODEL), wq, wk, wv, wg, cos, sin)


def _ret_decay_tables(c):
    lg = jnp.log(1.0 - 2.0 ** (-5.0 - jnp.arange(RET_HEADS, dtype=jnp.float32)))
    idx = jnp.arange(c, dtype=jnp.float32)
    diff = idx[:, None] - idx[None, :]
    d_intra = jnp.where(diff[None] >= 0.0, jnp.exp(jnp.maximum(diff, 0.0)[None] * lg[:, None, None]), 0.0)
    q_dec = jnp.exp((idx[:, None] + 1.0) * lg[None, :])
    k_dec = jnp.exp((c - 1.0 - idx)[:, None] * lg[None, :])
    c_dec = jnp.exp(c * lg)
    return d_intra, q_dec.T[:, :, None], k_dec.T[:, :, None], c_dec[:, None, None]


def _ret_core_body(*refs, c, has_state):
    if has_state:
        q_ref, k_ref, v_ref, sg_ref, d_ref, qd_ref, kd_ref, cd_ref, s0_ref, o_ref, so_ref, s_ref = refs
    else:
        q_ref, k_ref, v_ref, sg_ref, d_ref, qd_ref, kd_ref, cd_ref, o_ref, so_ref, s_ref = refs
    i = pl.program_id(2)

    @pl.when(i == 0)
    def _():
        s_ref[...] = s0_ref[...] if has_state else jnp.zeros(s_ref.shape, F32)

    q, k, v = q_ref[...], k_ref[...], v_ref[...]
    state = s_ref[...]
    att = _dot_nt(q, k) * d_ref[...]
    o = _dot(att, v) + _dot(q * qd_ref[...], state)
    kd = k * kd_ref[...]
    if c < LANES:
        kd = jnp.concatenate([kd, jnp.zeros((LANES - c, kd.shape[1]), F32)], axis=0)
        v = jnp.concatenate([v, jnp.zeros((LANES - c, v.shape[1]), v.dtype)], axis=0)
    s_new = state * cd_ref[...] + _dot(kd.T, v)
    s_ref[...] = s_new
    o_ref[...] = _rms(o) * sg_ref[...]

    @pl.when(i == pl.num_programs(2) - 1)
    def _():
        so_ref[...] = s_new


def _ret_core(q, k, v, sg, s0, batch, seq):
    c = min(RET_CHUNK, seq)
    nc = seq // c
    has_state = s0 is not None
    d_intra, q_dec, k_dec, c_dec = _ret_decay_tables(c)
    tok = lambda w: pl.BlockSpec((c, w), lambda b, h, i: (b * nc + i, h))
    head = lambda s: pl.BlockSpec((None,) + s, lambda b, h, i: (h, 0, 0))
    st = pl.BlockSpec((None, None, RET_KEY_DIM, RET_VALUE_DIM), lambda b, h, i: (b, h, 0, 0))
    in_specs = [tok(RET_KEY_DIM), tok(RET_KEY_DIM), tok(RET_VALUE_DIM), tok(RET_VALUE_DIM),
                head((c, c)), head((c, 1)), head((c, 1)), head((1, 1))]
    args = [q, k, v, sg, d_intra, q_dec, k_dec, c_dec]
    if has_state:
        in_specs.append(st)
        args.append(s0)
    return pl.pallas_call(
        functools.partial(_ret_core_body, c=c, has_state=has_state),
        grid=(batch, RET_HEADS, nc),
        in_specs=in_specs,
        out_specs=[tok(RET_VALUE_DIM), st],
        out_shape=[jax.ShapeDtypeStruct((batch * seq, RET_HEADS * RET_VALUE_DIM), F32),
                   jax.ShapeDtypeStruct((batch, RET_HEADS, RET_KEY_DIM, RET_VALUE_DIM), F32)],
        scratch_shapes=[pltpu.VMEM((RET_KEY_DIM, RET_VALUE_DIM), F32)],
        compiler_params=_params(("parallel", "parallel", "arbitrary")),
        name="ret_core",
    )(*args)


def _norm_body(x_ref, g_ref, o_ref):
    o_ref[...] = _rms(x_ref[...], g_ref[...])


def _norm(x, g):
    n = x.shape[0]
    tm = _tile(n)
    return pl.pallas_call(
        _norm_body,
        grid=(n // tm,),
        in_specs=[_row_spec(tm, D_MODEL), _const_spec((1, D_MODEL))],
        out_specs=_row_spec(tm, D_MODEL),
        out_shape=jax.ShapeDtypeStruct((n, D_MODEL), F32),
        compiler_params=_params(("parallel",)),
        name="norm",
    )(x, g.reshape(1, D_MODEL))


def _rwkv_pre_body(h_ref, hp_ref, mix_ref, wr_ref, wk_ref, wv_ref, w1_ref, w2_ref, a1_ref, a2_ref, g1_ref, g2_ref,
                   vec_ref, bd_ref, r_ref, lw_ref, k_ref, v_ref, na_ref, bb_ref, g_ref):
    h = h_ref[...]
    xx = hp_ref[...] - h
    xr, xw, xk, xv, xa, xg = [h + xx * mix_ref[m:m + 1, :] for m in range(6)]
    w0, a0, k_k, k_a = [vec_ref[m:m + 1, :] for m in range(4)]
    r_ref[...] = _dot(xr, wr_ref[...])
    w_log = -jax.nn.softplus(-(w0 + _dot(jnp.tanh(_dot(xw, w1_ref[...])), w2_ref[...]))) - 0.5
    lw_ref[...] = -jnp.exp(w_log)
    k = _dot(xk, wk_ref[...])
    v_ref[...] = _dot(xv, wv_ref[...])
    a = jax.nn.sigmoid(a0 + _dot(_dot(xa, a1_ref[...]), a2_ref[...]))
    g_ref[...] = _dot(jax.nn.sigmoid(_dot(xg, g1_ref[...])), g2_ref[...])
    kk = k * k_k
    kk = kk / jnp.maximum(jnp.sqrt(_group_sum(kk * kk, bd_ref)), 1e-12)
    k_ref[...] = k * (1.0 + (a - 1.0) * k_a)
    na_ref[...] = -kk
    bb_ref[...] = kk * a


def _rwkv_pre(h, hp, mix, wr, wk, wv, w1, w2, a1, a2, g1, g2, vecs):
    n = h.shape[0]
    tm = min(256, n)
    assert n % tm == 0
    row = _row_spec(tm, D_MODEL)
    ws = [wr, wk, wv, w1, w2, a1, a2, g1, g2]
    return pl.pallas_call(
        _rwkv_pre_body,
        grid=(n // tm,),
        in_specs=[row, row, _const_spec(mix.shape)] + [_const_spec(w.shape) for w in ws]
        + [_const_spec(vecs.shape), _const_spec((MXU_DIM, MXU_DIM))],
        out_specs=[row] * 7,
        out_shape=[jax.ShapeDtypeStruct((n, D_MODEL), F32)] * 7,
        compiler_params=_params(("parallel",)),
        name="rwkv_pre",
    )(h, hp, mix, *ws, vecs, _block_diag_ones(RWKV_HEAD_DIM))


def _rwkv_scan_body(*refs, c, has_state):
    if has_state:
        r_ref, lw_ref, k_ref, v_ref, na_ref, bb_ref, s0_ref, y_ref, so_ref, s_ref = refs
    else:
        r_ref, lw_ref, k_ref, v_ref, na_ref, bb_ref, y_ref, so_ref, s_ref = refs
    i = pl.program_id(1)
    c2 = 2 * c

    @pl.when(i == 0)
    def _():
        s_ref[...] = s0_ref[...] if has_state else jnp.zeros(s_ref.shape, F32)

    row = lax.broadcasted_iota(jnp.int32, (c, c), 0)
    col = lax.broadcasted_iota(jnp.int32, (c, c), 1)
    tril = (col <= row).astype(BF16)
    row2 = lax.broadcasted_iota(jnp.int32, (c2, c2), 0)
    col2 = lax.broadcasted_iota(jnp.int32, (c2, c2), 1)
    strict = col2 < row2
    incl = col2 <= row2
    eye = (col2 == row2).astype(F32)
    first = lax.broadcasted_iota(jnp.int32, (c, LANES), 1) < RWKV_HEAD_DIM

    def stack(z):
        zero = jnp.zeros_like(z)
        return jnp.concatenate([jnp.where(first, z, zero), jnp.where(first, zero, z)], axis=0)

    for p in range(RWKV_HEADS // 2):
        sl = slice(p * LANES, (p + 1) * LANES)
        r, lw, k, v, na, bb = (ref[:, sl] for ref in (r_ref, lw_ref, k_ref, v_ref, na_ref, bb_ref))
        lw_hi = lw.astype(BF16)
        lw_lo = (lw - lw_hi.astype(F32)).astype(BF16)
        cs = (jnp.dot(tril, lw_hi, preferred_element_type=F32) + jnp.dot(tril, lw_lo, preferred_element_type=F32))
        total = cs[c - 1:c, :]
        p_in = jnp.exp(cs)
        p_inv = jnp.exp(-cs)
        p_rest = jnp.exp(total - cs)
        left = jnp.concatenate([stack(na * jnp.exp(cs - lw)), stack(r * p_in)], axis=0)
        right = jnp.concatenate([stack(bb * p_inv), stack(k * p_inv)], axis=0)
        a_ab = jnp.where(strict, _dot_nt(left[0:c2, :], right[0:c2, :]), 0.0)
        a_ak = jnp.where(strict, _dot_nt(left[0:c2, :], right[c2:2 * c2, :]), 0.0)
        a_rb = jnp.where(incl, _dot_nt(left[c2:2 * c2, :], right[0:c2, :]), 0.0)
        a_rk = jnp.where(incl, _dot_nt(left[c2:2 * c2, :], right[c2:2 * c2, :]), 0.0)
        power = a_ab
        inv = eye + a_ab
        step = 2
        while step < c:
            power = _dot(power, power)
            inv = inv + _dot(inv, power)
            step *= 2
        v2 = stack(v)
        w_u0 = _dot(inv, jnp.concatenate([left[0:c2, :], _dot(a_ak, v2)], axis=1))
        state = s_ref[p]
        proj = _dot_nt(jnp.concatenate([w_u0[:, 0:LANES], left[c2:2 * c2, :]], axis=0), state)
        u = proj[0:c2, :] + w_u0[:, LANES:2 * LANES]
        uv = jnp.concatenate([u, v2], axis=0)
        y2 = proj[c2:2 * c2, :] + _dot(a_rb, u) + _dot(a_rk, v2)
        y_ref[:, sl] = y2[0:c, :] + y2[c:c2, :]
        bk = jnp.concatenate([stack(bb * p_rest), stack(k * p_rest)], axis=0)
        if 2 * c2 < LANES:
            pad = jnp.zeros((LANES - 2 * c2, LANES), F32)
            uv = jnp.concatenate([uv, pad], axis=0)
            bk = jnp.concatenate([bk, pad], axis=0)
        s_ref[p] = state * jnp.exp(total) + _dot(uv.T, bk)

    @pl.when(i == pl.num_programs(1) - 1)
    def _():
        so_ref[...] = s_ref[...]


def _rwkv_scan(r, lw, k, v, na, bb, s0, batch, seq):
    c = min(RWKV_CHUNK, seq)
    nc = seq // c
    has_state = s0 is not None
    tok = pl.BlockSpec((c, D_MODEL), lambda b, i: (b * nc + i, 0))
    st = pl.BlockSpec((None, RWKV_HEADS // 2, LANES, LANES), lambda b, i: (b, 0, 0, 0))
    args = [r, lw, k, v, na, bb] + ([s0] if has_state else [])
    return pl.pallas_call(
        functools.partial(_rwkv_scan_body, c=c, has_state=has_state),
        grid=(batch, nc),
        in_specs=[tok] * 6 + ([st] if has_state else []),
        out_specs=[tok, st],
        out_shape=[jax.ShapeDtypeStruct((batch * seq, D_MODEL), F32),
                   jax.ShapeDtypeStruct((batch, RWKV_HEADS // 2, LANES, LANES), F32)],
        scratch_shapes=[pltpu.VMEM((RWKV_HEADS // 2, LANES, LANES), F32)],
        compiler_params=_params(("parallel", "arbitrary")),
        name="rwkv_scan",
    )(*args)


def _pair_states(s):
    b = s.shape[0]
    s = s.reshape(b, RWKV_HEADS // 2, 2, RWKV_HEAD_DIM, RWKV_HEAD_DIM)
    z = jnp.zeros_like(s[:, :, 0])
    top = jnp.concatenate([s[:, :, 0], z], axis=-1)
    bot = jnp.concatenate([z, s[:, :, 1]], axis=-1)
    return jnp.concatenate([top, bot], axis=-2)


def _unpair_states(s):
    b = s.shape[0]
    n = RWKV_HEAD_DIM
    return jnp.stack([s[:, :, :n, :n], s[:, :, n:, n:]], axis=2).reshape(b, RWKV_HEADS, n, n)


def _rwkv_post_body(x_ref, y_ref, r_ref, k_ref, v_ref, g_ref, vec_ref, bd_ref, wo_ref, o_ref):
    y = y_ref[...]
    r_k, ln_w, ln_b = [vec_ref[m:m + 1, :] for m in range(3)]
    inv_n = 1.0 / RWKV_HEAD_DIM
    mu = _group_sum(y, bd_ref, split=True) * inv_n
    yc = y - mu
    var = _group_sum(yc * yc, bd_ref) * inv_n
    yn = yc * lax.rsqrt(var + RWKV_GN_EPS) * ln_w + ln_b
    bonus = _group_sum(r_ref[...] * k_ref[...] * r_k, bd_ref, split=True) * v_ref[...]
    o_ref[...] = x_ref[...] + _dot((yn + bonus) * g_ref[...], wo_ref[...])


def _rwkv_post(x, y, r, k, v, g, vecs, wo):
    n = x.shape[0]
    tm = min(256, n)
    assert n % tm == 0
    row = _row_spec(tm, D_MODEL)
    return pl.pallas_call(
        _rwkv_post_body,
        grid=(n // tm,),
        in_specs=[row] * 6 + [_const_spec(vecs.shape), _const_spec((MXU_DIM, MXU_DIM)), _const_spec((D_MODEL, D_MODEL))],
        out_specs=row,
        out_shape=jax.ShapeDtypeStruct((n, D_MODEL), F32),
        compiler_params=_params(("parallel",)),
        name="rwkv_post",
    )(x, y, r, k, v, g, vecs, _block_diag_ones(RWKV_HEAD_DIM), wo)


def _attn_layer(x, pos, batch, seq, i, w, cache=None):
    lam_init = 0.8 - 0.6 * math.exp(-0.3 * i)
    q, k, kb, v, vb = _attn_proj(x, w["g"], w["wq"], w["wk"], w["wv"], w["qk_norm"], pos)
    if cache is None:
        o = _attn_prompt(q, kb, vb, w["lam"], w["subln"], lam_init, batch, seq)
    else:
        o = _attn_sample(q, kb, vb, *cache, w["lam"], w["subln"], lam_init, seq)
    shape = (batch, seq, ATT_HEADS, 2 * ATT_HEAD_DIM)
    return _out_proj(x, o, w["wo"]), k.reshape(shape), v.reshape(shape)


def _ret_layer(x, pos, batch, seq, w, s0):
    q, k, v, sg = _ret_proj(x, w["g"], w["wq"], w["wk"], w["wv"], w["wg"], pos)
    og, s = _ret_core(q, k, v, sg, s0, batch, seq)
    return _out_proj(x, og, w["wo"]), s


def _rwkv_layer(x, batch, seq, w, shift, s0):
    h = _norm(x, w["g"])
    h3 = h.reshape(batch, seq, D_MODEL)
    hp = jnp.concatenate([shift[:, None, :], h3[:, :-1]], axis=1).reshape(batch * seq, D_MODEL)
    r, lw, k, v, na, bb, g = _rwkv_pre(h, hp, w["mix"], w["wr"], w["wk"], w["wv"], w["w1"], w["w2"], w["a1"], w["a2"],
                                       w["g1"], w["g2"], w["pre_vecs"])
    y, s = _rwkv_scan(r, lw, k, v, na, bb, None if s0 is None else _pair_states(s0), batch, seq)
    out = _rwkv_post(x, y, r, k, v, g, w["post_vecs"], w["wo"])
    return out, _unpair_states(s), h3[:, -1]


def kernel(x_prompt, x_sample, cache_k, cache_v, page_table, state_ret, state_rwkv, state_rwkv_shift, norm_g, ffn_w_gate, ffn_w_up, ffn_w_down, attn_w_q, attn_w_k, attn_w_v, attn_w_o, attn_qk_norm, attn_lambda, attn_subln, ret_w_q, ret_w_k, ret_w_v, ret_w_g, ret_w_o, rwkv_mix, rwkv_w_r, rwkv_w_k, rwkv_w_v, rwkv_w_o, rwkv_w0, rwkv_w1, rwkv_w2, rwkv_a0, rwkv_a1, rwkv_a2, rwkv_g1, rwkv_g2, rwkv_k_k, rwkv_k_a, rwkv_r_k, rwkv_ln_w, rwkv_ln_b):
    b_p, t_p, d = x_prompt.shape
    b_s, t_s, _ = x_sample.shape
    depth = norm_g.shape[0]
    past = page_table.shape[1] * cache_k.shape[2]
    pos_p = jnp.tile(jnp.arange(t_p, dtype=jnp.float32), b_p)
    pos_s = jnp.tile(past + jnp.arange(t_s, dtype=jnp.float32), b_s)
    yp = x_prompt.reshape(b_p * t_p, d)
    ys = x_sample.reshape(b_s * t_s, d)
    bf = lambda a: a.astype(BF16)
    outs = {name: [] for name in ("akp", "avp", "aks", "avs", "rsp", "rss", "wsp", "wss", "shp", "shs")}
    for i in range(depth):
        kind, j = i % 3, i // 3
        wg, wu, wd = bf(ffn_w_gate[i]), bf(ffn_w_up[i]), bf(ffn_w_down[i])
        yp = _ffn(yp, norm_g[i, 0], wg[0], wu[0], wd[0])
        ys = _ffn(ys, norm_g[i, 0], wg[0], wu[0], wd[0])
        if kind == 0:
            w = dict(g=norm_g[i, 1], wq=bf(attn_w_q[j]), wk=bf(attn_w_k[j]), wv=bf(attn_w_v[j]), wo=bf(attn_w_o[j]),
                     qk_norm=attn_qk_norm[j], lam=attn_lambda[j], subln=attn_subln[j])
            yp, kp, vp = _attn_layer(yp, pos_p, b_p, t_p, i, w)
            ys, ks, vs = _attn_layer(ys, pos_s, b_s, t_s, i, w, cache=(cache_k, cache_v, page_table, j))
            outs["akp"].append(kp)
            outs["avp"].append(vp)
            outs["aks"].append(ks)
            outs["avs"].append(vs)
        elif kind == 1:
            w = dict(g=norm_g[i, 1], wq=bf(ret_w_q[j]), wk=bf(ret_w_k[j]), wv=bf(ret_w_v[j]), wg=bf(ret_w_g[j]),
                     wo=bf(ret_w_o[j]))
            yp, sp = _ret_layer(yp, pos_p, b_p, t_p, w, None)
            ys, ss = _ret_layer(ys, pos_s, b_s, t_s, w, state_ret[j])
            outs["rsp"].append(sp)
            outs["rss"].append(ss)
        else:
            w = dict(g=norm_g[i, 1], mix=rwkv_mix[j], wr=bf(rwkv_w_r[j]), wk=bf(rwkv_w_k[j]), wv=bf(rwkv_w_v[j]),
                     wo=bf(rwkv_w_o[j]), w1=bf(rwkv_w1[j]), w2=bf(rwkv_w2[j]), a1=bf(rwkv_a1[j]), a2=bf(rwkv_a2[j]),
                     g1=bf(rwkv_g1[j]), g2=bf(rwkv_g2[j]),
                     pre_vecs=jnp.stack([rwkv_w0[j], rwkv_a0[j], rwkv_k_k[j], rwkv_k_a[j]]),
                     post_vecs=jnp.stack([rwkv_r_k[j].reshape(d), rwkv_ln_w[j], rwkv_ln_b[j]]))
            yp, sp, lp = _rwkv_layer(yp, b_p, t_p, w, jnp.zeros((b_p, d), F32), None)
            ys, ss, ls = _rwkv_layer(ys, b_s, t_s, w, state_rwkv_shift[j], state_rwkv[j])
            outs["wsp"].append(sp)
            outs["wss"].append(ss)
            outs["shp"].append(lp)
            outs["shs"].append(ls)
        yp = _ffn(yp, norm_g[i, 2], wg[1], wu[1], wd[1])
        ys = _ffn(ys, norm_g[i, 2], wg[1], wu[1], wd[1])
    return (yp.reshape(b_p, t_p, d), ys.reshape(b_s, t_s, d)) + tuple(
        jnp.stack(outs[name]) for name in ("akp", "avp", "aks", "avs", "rsp", "rss", "wsp", "wss", "shp", "shs"))
```

```python
import functools
import math

import jax
import jax.numpy as jnp
from jax import lax
from jax.experimental import pallas as pl
from jax.experimental.pallas import tpu as pltpu

F32 = jnp.float32
BF16 = jnp.bfloat16

D_MODEL = 1024
D_FF = 2816
NORM_EPS = 1e-6
ATT_HEAD_DIM = 64
ATT_HEADS = 8
ATT_SCALE = ATT_HEAD_DIM ** -0.5
ROT_DIM = 16
ROPE_THETA = 500000.0
RET_HEADS = 4
RET_KEY_DIM = 256
RET_VALUE_DIM = 512
RET_CHUNK = 128
RET_THETA = 10000.0
RWKV_HEAD_DIM = 64
RWKV_HEADS = 16
RWKV_GN_EPS = 64e-5
RWKV_CHUNK = 64

LANES = 128
MXU_DIM = 256
VMEM_LIMIT = 56 * 1024 * 1024
TOKEN_TILE = 512
FF_CHUNK = 256


def _params(sem):
    return pltpu.CompilerParams(dimension_semantics=sem, vmem_limit_bytes=VMEM_LIMIT)


def _dot(a, b):
    return jnp.dot(a.astype(BF16), b.astype(BF16), preferred_element_type=F32)


def _dot_nt(a, b):
    return lax.dot_general(a.astype(BF16), b.astype(BF16), (((1,), (1,)), ((), ())),
                           preferred_element_type=F32)


def _rms(x, gain=None, eps=NORM_EPS):
    y = x * lax.rsqrt(jnp.mean(x * x, axis=-1, keepdims=True) + eps)
    return y if gain is None else y * gain


def _const_spec(shape):
    nd = len(shape)
    return pl.BlockSpec(shape, lambda *_: (0,) * nd)


def _row_spec(tm, width):
    return pl.BlockSpec((tm, width), lambda i: (i, 0))


def _weight(w):
    if not isinstance(w, tuple):
        return w, _const_spec(w.shape)
    arr, idx = w
    shape = arr.shape[len(idx):]
    tail = (0,) * len(shape)
    return arr, pl.BlockSpec((None,) * len(idx) + shape, lambda *_: idx + tail)


def _tile(n):
    tm = min(TOKEN_TILE, n)
    assert n % tm == 0
    return tm


def _block_diag_ones(group):
    idx = jnp.arange(MXU_DIM) // group
    return (idx[:, None] == idx[None, :]).astype(BF16)


def _group_sum(x, bd_ref, split=False):
    parts = []
    for c in range(x.shape[1] // MXU_DIM):
        xc = x[:, c * MXU_DIM:(c + 1) * MXU_DIM]
        hi = xc.astype(BF16)
        s = jnp.dot(hi, bd_ref[...], preferred_element_type=F32)
        if split:
            lo = (xc - hi.astype(F32)).astype(BF16)
            s = s + jnp.dot(lo, bd_ref[...], preferred_element_type=F32)
        parts.append(s)
    return jnp.concatenate(parts, axis=1)


def _ffn_body(x_ref, g_ref, wg_ref, wu_ref, wd_ref, o_ref):
    x = x_ref[...]
    h = _rms(x, g_ref[...]).astype(BF16)
    acc = jnp.zeros(x.shape, F32)
    for c in range(D_FF // FF_CHUNK):
        sl = slice(c * FF_CHUNK, (c + 1) * FF_CHUNK)
        gate = jnp.dot(h, wg_ref[:, sl], preferred_element_type=F32)
        up = jnp.dot(h, wu_ref[:, sl], preferred_element_type=F32)
        act = (gate * jax.nn.sigmoid(gate) * up).astype(BF16)
        acc = acc + jnp.dot(act, wd_ref[sl, :], preferred_element_type=F32)
    o_ref[...] = x + 0.5 * acc


def _ffn(x, g, wg, wu, wd):
    n = x.shape[0]
    tm = _tile(n)
    (wg, wg_spec), (wu, wu_spec), (wd, wd_spec) = _weight(wg), _weight(wu), _weight(wd)
    return pl.pallas_call(
        _ffn_body,
        grid=(n // tm,),
        in_specs=[_row_spec(tm, D_MODEL), _const_spec((1, D_MODEL)), wg_spec, wu_spec, wd_spec],
        out_specs=_row_spec(tm, D_MODEL),
        out_shape=jax.ShapeDtypeStruct((n, D_MODEL), F32),
        compiler_params=_params(("parallel",)),
        name="ffn",
    )(x, g.reshape(1, D_MODEL), wg, wu, wd)


def _out_proj_body(x_ref, a_ref, w_ref, o_ref):
    o_ref[...] = x_ref[...] + jnp.dot(a_ref[...].astype(BF16), w_ref[...], preferred_element_type=F32)


def _out_proj(x, a, w):
    n, k = a.shape
    tm = _tile(n)
    w, w_spec = _weight(w)
    return pl.pallas_call(
        _out_proj_body,
        grid=(n // tm,),
        in_specs=[_row_spec(tm, D_MODEL), _row_spec(tm, k), w_spec],
        out_specs=_row_spec(tm, D_MODEL),
        out_shape=jax.ShapeDtypeStruct((n, D_MODEL), F32),
        compiler_params=_params(("parallel",)),
        name="out_proj",
    )(x, a, w)


def _attn_rot_tables(pos):
    half = ROT_DIM // 2
    inv_freq = ROPE_THETA ** (-jnp.arange(half, dtype=jnp.float32) * 2.0 / ROT_DIM)
    ang = pos[:, None] * inv_freq[None, :]
    cos, sin = jnp.cos(ang), jnp.sin(ang)
    n = pos.shape[0]
    ones = jnp.ones((n, ATT_HEAD_DIM - ROT_DIM), jnp.float32)
    zeros = jnp.zeros((n, ATT_HEAD_DIM - ROT_DIM), jnp.float32)
    zh = jnp.zeros((n, half), jnp.float32)
    c = jnp.concatenate([cos, cos, ones], axis=1)
    s_up = jnp.concatenate([-sin, zh, zeros], axis=1)
    s_dn = jnp.concatenate([zh, sin, zeros], axis=1)
    rep = LANES // ATT_HEAD_DIM
    return tuple(jnp.tile(t, (1, rep)) for t in (c, s_up, s_dn))


def _attn_proj_body(x_ref, g_ref, wq_ref, wk_ref, wv_ref, qkn_ref, bd_ref, c_ref, su_ref, sd_ref,
                    q_ref, k_ref, kb_ref, v_ref, vt_ref):
    h = _rms(x_ref[...], g_ref[...]).astype(BF16)
    rep = D_MODEL // LANES
    cos = jnp.concatenate([c_ref[...]] * rep, axis=1)
    s_up = jnp.concatenate([su_ref[...]] * rep, axis=1)
    s_dn = jnp.concatenate([sd_ref[...]] * rep, axis=1)
    half = ROT_DIM // 2

    def norm_rot(z, gain):
        ms = _group_sum(z * z, bd_ref) * (1.0 / ATT_HEAD_DIM)
        z = z * lax.rsqrt(ms + NORM_EPS) * gain
        up = pltpu.roll(z, D_MODEL - half, 1)
        dn = pltpu.roll(z, half, 1)
        return z * cos + up * s_up + dn * s_dn

    q = norm_rot(jnp.dot(h, wq_ref[...], preferred_element_type=F32), qkn_ref[0:1, :])
    q_ref[...] = (q * ATT_SCALE).astype(BF16)
    k = norm_rot(jnp.dot(h, wk_ref[...], preferred_element_type=F32), qkn_ref[1:2, :])
    k_ref[...] = k
    kb_ref[...] = k.astype(BF16)
    v = jnp.dot(h, wv_ref[...], preferred_element_type=F32)
    v_ref[...] = v
    vt_ref[...] = v.T.astype(BF16)


def _attn_proj(x, g, wq, wk, wv, qk_norm, pos):
    n = x.shape[0]
    tm = _tile(n)
    qkn = jnp.tile(qk_norm, (1, D_MODEL // ATT_HEAD_DIM))
    tabs = _attn_rot_tables(pos)
    (wq, wq_spec), (wk, wk_spec), (wv, wv_spec) = _weight(wq), _weight(wk), _weight(wv)
    row = _row_spec(tm, D_MODEL)
    tab = _row_spec(tm, LANES)
    f = jax.ShapeDtypeStruct((n, D_MODEL), F32)
    b = jax.ShapeDtypeStruct((n, D_MODEL), BF16)
    return pl.pallas_call(
        _attn_proj_body,
        grid=(n // tm,),
        in_specs=[row, _const_spec((1, D_MODEL)), wq_spec, wk_spec, wv_spec, _const_spec((2, D_MODEL)),
                  _const_spec((MXU_DIM, MXU_DIM)), tab, tab, tab],
        out_specs=[row] * 4 + [pl.BlockSpec((None, D_MODEL, tm), lambda i: (i, 0, 0))],
        out_shape=[b, f, b, f, jax.ShapeDtypeStruct((n // tm, D_MODEL, tm), BF16)],
        compiler_params=_params(("parallel",)),
        name="attn_proj",
    )(x, g.reshape(1, D_MODEL), wq, wk, wv, qkn, _block_diag_ones(ATT_HEAD_DIM), *tabs)


def _half_masks(rows):
    lane = lax.broadcasted_iota(jnp.int32, (rows, LANES), 1)
    return lane < ATT_HEAD_DIM


def _stack_halves(q):
    first = _half_masks(q.shape[0])
    zero = jnp.zeros_like(q)
    return jnp.concatenate([jnp.where(first, q, zero), jnp.where(first, zero, q)], axis=0)


def _softmax_step(s, v, m_ref, l_ref, acc_ref):
    m_prev = m_ref[...]
    m_new = jnp.maximum(m_prev, jnp.max(s, axis=-1, keepdims=True))
    alpha = jnp.exp(m_prev - m_new)
    p = jnp.exp(s - m_new[:, 0:1])
    l_ref[...] = alpha * l_ref[...] + jnp.sum(p, axis=-1, keepdims=True)
    acc_ref[...] = alpha * acc_ref[...] + jnp.dot(p.astype(BF16), v, preferred_element_type=F32)
    m_ref[...] = m_new


def _diff_lambda(lam_ref, lam_init):
    lp = lam_ref[...]
    return (jnp.exp(jnp.sum(lp[0:1] * lp[1:2], axis=-1, keepdims=True))
            - jnp.exp(jnp.sum(lp[2:3] * lp[3:4], axis=-1, keepdims=True)) + lam_init)


def _attn_prompt_body(q_ref, k_ref, vt_ref, lam_ref, subln_ref, o_ref, m_ref, l_ref, acc_ref, s_ref, *, tq, lam_init):
    i = pl.program_id(2)
    q2 = _stack_halves(q_ref[...])
    m_ref[...] = jnp.full(m_ref.shape, -jnp.inf, F32)
    l_ref[...] = jnp.zeros(l_ref.shape, F32)
    acc_ref[...] = jnp.zeros(acc_ref.shape, F32)

    def scores(j):
        off = pl.multiple_of(j * tq, tq)
        return _dot_nt(k_ref[pl.ds(off, tq), :], q2)

    def step(j, slot, masked):
        st = s_ref[slot]
        if masked:
            key = lax.broadcasted_iota(jnp.int32, (tq, 2 * tq), 0)
            qry = lax.broadcasted_iota(jnp.int32, (tq, 2 * tq), 1)
            qry = jnp.where(qry >= tq, qry - tq, qry)
            st = jnp.where(key <= qry, st, -jnp.inf)
        m_prev = m_ref[...]
        m_new = jnp.maximum(m_prev, jnp.max(st, axis=0, keepdims=True))
        alpha = jnp.exp(m_prev - m_new)
        p = jnp.exp(st - m_new)
        l_ref[...] = alpha * l_ref[...] + jnp.sum(p, axis=0, keepdims=True)
        acc_ref[...] = alpha * acc_ref[...] + jnp.dot(vt_ref[j], p.astype(BF16), preferred_element_type=F32)
        m_ref[...] = m_new

    s_ref[0] = scores(0)

    def two_blocks(jj, carry):
        j = 2 * jj
        s_ref[1] = scores(j + 1)
        step(j, 0, False)
        s_ref[0] = scores(j + 2)
        step(j + 1, 1, False)
        return carry

    lax.fori_loop(0, i // 2, two_blocks, 0)

    @pl.when(i % 2 == 0)
    def _():
        step(i, 0, True)

    @pl.when(i % 2 == 1)
    def _():
        s_ref[1] = scores(i)
        step(i - 1, 0, False)
        step(i, 1, True)

    inv_l = 1.0 / l_ref[...]
    lam = _diff_lambda(lam_ref, lam_init)
    ot = acc_ref[:, 0:tq] * inv_l[:, 0:tq] - lam * (acc_ref[:, tq:2 * tq] * inv_l[:, tq:2 * tq])
    o_ref[...] = _rms(ot.T, subln_ref[...]) * (1.0 - lam_init)


def _attn_prompt(q, k, vt, lam_params, subln, lam_init, batch, seq):
    tq = vt.shape[2]
    nq = seq // tq
    assert seq % tq == 0
    body = functools.partial(_attn_prompt_body, tq=tq, lam_init=lam_init)
    return pl.pallas_call(
        body,
        grid=(batch, ATT_HEADS, nq),
        in_specs=[pl.BlockSpec((tq, LANES), lambda b, h, i: (b * nq + i, h)),
                  pl.BlockSpec((seq, LANES), lambda b, h, i: (b, h)),
                  pl.BlockSpec((nq, LANES, tq), lambda b, h, i: (b, h, 0)),
                  _const_spec((4, ATT_HEAD_DIM)), _const_spec((1, LANES))],
        out_specs=pl.BlockSpec((tq, LANES), lambda b, h, i: (b * nq + i, h)),
        out_shape=jax.ShapeDtypeStruct((batch * seq, D_MODEL), F32),
        scratch_shapes=[pltpu.VMEM((1, 2 * tq), F32), pltpu.VMEM((1, 2 * tq), F32), pltpu.VMEM((LANES, 2 * tq), F32),
                        pltpu.VMEM((2, tq, 2 * tq), F32)],
        compiler_params=_params(("parallel", "parallel", "parallel")),
        name="attn_prompt",
    )(q, k, vt, lam_params, subln.reshape(1, LANES))


def _attn_sample_body(pt_ref, q_ref, kn_ref, vn_ref, *refs, t, group, lam_init):
    del pt_ref
    ck_refs, cv_refs = refs[:group], refs[group:2 * group]
    lam_ref, subln_ref, o_ref, q2_ref, bias_ref, m_ref, l_ref, acc_ref = refs[2 * group:]
    p = pl.program_id(1)
    rows = ATT_HEADS * 2 * t
    head_shift = (2 * t).bit_length() - 1
    tok_shift = ATT_HEADS.bit_length() - 1

    def head_match(shape):
        row = lax.broadcasted_iota(jnp.int32, shape, 0)
        col = lax.broadcasted_iota(jnp.int32, shape, 1)
        return row, col, jnp.bitwise_and(col, ATT_HEADS - 1) == jnp.right_shift(row, head_shift)

    @pl.when(p == 0)
    def _():
        for h in range(ATT_HEADS):
            q2_ref[h * 2 * t:(h + 1) * 2 * t, :] = _stack_halves(q_ref[:, h * LANES:(h + 1) * LANES])
        _, _, ok = head_match(bias_ref.shape)
        bias_ref[...] = jnp.where(ok, 0.0, -jnp.inf)
        m_ref[...] = jnp.full(m_ref.shape, -jnp.inf, F32)
        l_ref[...] = jnp.zeros(l_ref.shape, F32)
        acc_ref[...] = jnp.zeros(acc_ref.shape, F32)

    q2 = q2_ref[...]
    s = jnp.concatenate([_dot_nt(q2, ck[...]) + bias_ref[...] for ck in ck_refs], axis=1)
    m_prev = m_ref[...]
    m_new = jnp.maximum(m_prev, jnp.max(s, axis=-1, keepdims=True))
    alpha = jnp.exp(m_prev - m_new)
    prob = jnp.exp(s - m_new[:, 0:1])
    l_ref[...] = alpha * l_ref[...] + jnp.sum(prob, axis=-1, keepdims=True)
    width = bias_ref.shape[1]
    pv = [_dot(prob[:, g * width:(g + 1) * width], cv[...]) for g, cv in enumerate(cv_refs)]
    acc_ref[...] = alpha * acc_ref[...] + functools.reduce(lambda a, b: a + b, pv)
    m_ref[...] = m_new

    @pl.when(p == pl.num_programs(1) - 1)
    def _():
        row, col, ok = head_match((rows, t * ATT_HEADS))
        ok = jnp.logical_and(ok, jnp.right_shift(col, tok_shift) <= jnp.bitwise_and(row, t - 1))
        s_new = jnp.where(ok, _dot_nt(q2_ref[...], kn_ref[...]), -jnp.inf)
        _softmax_step(s_new, vn_ref[...], m_ref, l_ref, acc_ref)
        lam = _diff_lambda(lam_ref, lam_init)
        for h in range(ATT_HEADS):
            lo, mid, hi = h * 2 * t, h * 2 * t + t, (h + 1) * 2 * t
            o = acc_ref[lo:mid, :] / l_ref[lo:mid, :] - lam * (acc_ref[mid:hi, :] / l_ref[mid:hi, :])
            o_ref[:, h * LANES:(h + 1) * LANES] = _rms(o, subln_ref[...]) * (1.0 - lam_init)


def _attn_sample(q, kn, vn, cache_k, cache_v, page_table, layer, lam_params, subln, lam_init, t):
    batch, n_pages = page_table.shape
    page = cache_k.shape[2]
    rows = page * ATT_HEADS
    assert t & (t - 1) == 0 and ATT_HEADS & (ATT_HEADS - 1) == 0
    ck = cache_k.reshape(cache_k.shape[:2] + (rows, LANES))
    cv = cache_v.reshape(cache_v.shape[:2] + (rows, LANES))
    q = q.reshape(batch, t, D_MODEL)
    kn, vn = (z.reshape(batch, t * ATT_HEADS, LANES).astype(BF16) for z in (kn, vn))
    group = max(g for g in (4, 2, 1) if n_pages % g == 0)
    body = functools.partial(_attn_sample_body, t=t, group=group, lam_init=lam_init)
    tok = pl.BlockSpec((None, t, D_MODEL), lambda b, p, pt: (b, 0, 0))
    new = pl.BlockSpec((None, t * ATT_HEADS, LANES), lambda b, p, pt: (b, 0, 0))
    cache = [pl.BlockSpec((None, None, rows, LANES), lambda b, p, pt, g=g: (layer, pt[b, p * group + g], 0, 0))
             for g in range(group)]
    qrows = ATT_HEADS * 2 * t
    grid_spec = pltpu.PrefetchScalarGridSpec(
        num_scalar_prefetch=1,
        grid=(batch, n_pages // group),
        in_specs=[tok, new, new] + cache + cache + [
                  pl.BlockSpec((4, ATT_HEAD_DIM), lambda b, p, pt: (0, 0)),
                  pl.BlockSpec((1, LANES), lambda b, p, pt: (0, 0))],
        out_specs=tok,
        scratch_shapes=[pltpu.VMEM((qrows, LANES), BF16), pltpu.VMEM((qrows, rows), F32)]
        + [pltpu.VMEM((qrows, LANES), F32)] * 3,
    )
    return pl.pallas_call(
        body,
        grid_spec=grid_spec,
        out_shape=jax.ShapeDtypeStruct((batch, t, D_MODEL), F32),
        compiler_params=_params(("parallel", "arbitrary")),
        name="attn_sample",
    )(page_table, q, kn, vn, *([ck] * group), *([cv] * group), lam_params,
      subln.reshape(1, LANES)).reshape(batch * t, D_MODEL)


def _ret_rot_tables(pos):
    half = RET_KEY_DIM // 2
    inv_freq = RET_THETA ** (-jnp.arange(half, dtype=jnp.float32) * 2.0 / RET_KEY_DIM)
    ang = pos[:, None] * inv_freq[None, :]
    return jnp.cos(ang), jnp.sin(ang)


def _ret_proj_body(x_ref, g_ref, wq_ref, wk_ref, wv_ref, wg_ref, c_ref, s_ref, q_ref, k_ref, v_ref, sg_ref):
    h = _rms(x_ref[...], g_ref[...]).astype(BF16)
    cos, sin = c_ref[...], s_ref[...]

    def rot(z):
        parts = []
        for hd in range(RET_HEADS):
            x1 = z[:, hd * RET_KEY_DIM:hd * RET_KEY_DIM + LANES]
            x2 = z[:, hd * RET_KEY_DIM + LANES:(hd + 1) * RET_KEY_DIM]
            parts += [x1 * cos - x2 * sin, x2 * cos + x1 * sin]
        return jnp.concatenate(parts, axis=1)

    q_ref[...] = rot(jnp.dot(h, wq_ref[...], preferred_element_type=F32))
    k_ref[...] = rot(jnp.dot(h, wk_ref[...], preferred_element_type=F32) * (RET_KEY_DIM ** -0.5))
    v_ref[...] = jnp.dot(h, wv_ref[...], preferred_element_type=F32)
    gate = jnp.dot(h, wg_ref[...], preferred_element_type=F32)
    sg_ref[...] = gate * jax.nn.sigmoid(gate)


def _ret_proj(x, g, wq, wk, wv, wg, pos):
    n = x.shape[0]
    tm = _tile(n)
    cos, sin = _ret_rot_tables(pos)
    vw = RET_HEADS * RET_VALUE_DIM
    row = _row_spec(tm, D_MODEL)
    wide = _row_spec(tm, vw)
    tab = _row_spec(tm, LANES)
    ws = [_weight(w) for w in (wq, wk, wv, wg)]
    return pl.pallas_call(
        _ret_proj_body,
        grid=(n // tm,),
        in_specs=[row, _const_spec((1, D_MODEL))] + [spec for _, spec in ws] + [tab, tab],
        out_specs=[row, row, wide, wide],
        out_shape=[jax.ShapeDtypeStruct((n, D_MODEL), F32), jax.ShapeDtypeStruct((n, D_MODEL), F32),
                   jax.ShapeDtypeStruct((n, vw), F32), jax.ShapeDtypeStruct((n, vw), F32)],
        compiler_params=_params(("parallel",)),
        name="ret_proj",
    )(x, g.reshape(1, D_MODEL), *[w for w, _ in ws], cos, sin)


def _ret_decay_tables(c):
    lg = jnp.log(1.0 - 2.0 ** (-5.0 - jnp.arange(RET_HEADS, dtype=jnp.float32)))
    idx = jnp.arange(c, dtype=jnp.float32)
    diff = idx[:, None] - idx[None, :]
    d_intra = jnp.where(diff[None] >= 0.0, jnp.exp(jnp.maximum(diff, 0.0)[None] * lg[:, None, None]), 0.0)
    q_dec = jnp.exp((idx[:, None] + 1.0) * lg[None, :])
    k_dec = jnp.exp((c - 1.0 - idx)[:, None] * lg[None, :])
    c_dec = jnp.exp(c * lg)
    return d_intra, q_dec.T[:, :, None], k_dec.T[:, :, None], c_dec[:, None, None]


def _ret_core_body(*refs, c, has_state):
    if has_state:
        q_ref, k_ref, v_ref, sg_ref, d_ref, qd_ref, kd_ref, cd_ref, s0_ref, o_ref, so_ref, s_ref = refs
    else:
        q_ref, k_ref, v_ref, sg_ref, d_ref, qd_ref, kd_ref, cd_ref, o_ref, so_ref, s_ref = refs
    i = pl.program_id(2)

    @pl.when(i == 0)
    def _():
        s_ref[...] = s0_ref[...] if has_state else jnp.zeros(s_ref.shape, F32)

    q, k, v = q_ref[...], k_ref[...], v_ref[...]
    state = s_ref[...]
    att = _dot_nt(q, k) * d_ref[...]
    o = _dot(att, v) + _dot(q * qd_ref[...], state)
    kd = k * kd_ref[...]
    if c < LANES:
        kd = jnp.concatenate([kd, jnp.zeros((LANES - c, kd.shape[1]), F32)], axis=0)
        v = jnp.concatenate([v, jnp.zeros((LANES - c, v.shape[1]), v.dtype)], axis=0)
    s_new = state * cd_ref[...] + _dot(kd.T, v)
    s_ref[...] = s_new
    o_ref[...] = _rms(o) * sg_ref[...]

    @pl.when(i == pl.num_programs(2) - 1)
    def _():
        so_ref[...] = s_new


def _ret_core(q, k, v, sg, s0, batch, seq):
    c = min(RET_CHUNK, seq)
    nc = seq // c
    has_state = s0 is not None
    d_intra, q_dec, k_dec, c_dec = _ret_decay_tables(c)
    tok = lambda w: pl.BlockSpec((c, w), lambda b, h, i: (b * nc + i, h))
    head = lambda s: pl.BlockSpec((None,) + s, lambda b, h, i: (h, 0, 0))
    st = pl.BlockSpec((None, None, RET_KEY_DIM, RET_VALUE_DIM), lambda b, h, i: (b, h, 0, 0))
    in_specs = [tok(RET_KEY_DIM), tok(RET_KEY_DIM), tok(RET_VALUE_DIM), tok(RET_VALUE_DIM),
                head((c, c)), head((c, 1)), head((c, 1)), head((1, 1))]
    args = [q, k, v, sg, d_intra, q_dec, k_dec, c_dec]
    if has_state:
        in_specs.append(st)
        args.append(s0)
    return pl.pallas_call(
        functools.partial(_ret_core_body, c=c, has_state=has_state),
        grid=(batch, RET_HEADS, nc),
        in_specs=in_specs,
        out_specs=[tok(RET_VALUE_DIM), st],
        out_shape=[jax.ShapeDtypeStruct((batch * seq, RET_HEADS * RET_VALUE_DIM), F32),
                   jax.ShapeDtypeStruct((batch, RET_HEADS, RET_KEY_DIM, RET_VALUE_DIM), F32)],
        scratch_shapes=[pltpu.VMEM((RET_KEY_DIM, RET_VALUE_DIM), F32)],
        compiler_params=_params(("parallel", "parallel", "arbitrary")),
        name="ret_core",
    )(*args)


def _norm_body(x_ref, g_ref, o_ref):
    o_ref[...] = _rms(x_ref[...], g_ref[...])


def _norm(x, g):
    n = x.shape[0]
    tm = _tile(n)
    return pl.pallas_call(
        _norm_body,
        grid=(n // tm,),
        in_specs=[_row_spec(tm, D_MODEL), _const_spec((1, D_MODEL))],
        out_specs=_row_spec(tm, D_MODEL),
        out_shape=jax.ShapeDtypeStruct((n, D_MODEL), F32),
        compiler_params=_params(("parallel",)),
        name="norm",
    )(x, g.reshape(1, D_MODEL))


def _rwkv_pre_body(h_ref, hp_ref, mix_ref, wr_ref, wk_ref, wv_ref, w1_ref, w2_ref, a1_ref, a2_ref, g1_ref, g2_ref,
                   vec_ref, bd_ref, r_ref, lw_ref, k_ref, v_ref, na_ref, bb_ref, g_ref):
    h = h_ref[...]
    xx = hp_ref[...] - h
    xr, xw, xk, xv, xa, xg = [h + xx * mix_ref[m:m + 1, :] for m in range(6)]
    w0, a0, k_k, k_a = [vec_ref[m:m + 1, :] for m in range(4)]
    r_ref[...] = _dot(xr, wr_ref[...])
    w_log = -jax.nn.softplus(-(w0 + _dot(jnp.tanh(_dot(xw, w1_ref[...])), w2_ref[...]))) - 0.5
    lw_ref[...] = -jnp.exp(w_log)
    k = _dot(xk, wk_ref[...])
    v_ref[...] = _dot(xv, wv_ref[...])
    a = jax.nn.sigmoid(a0 + _dot(_dot(xa, a1_ref[...]), a2_ref[...]))
    g_ref[...] = _dot(jax.nn.sigmoid(_dot(xg, g1_ref[...])), g2_ref[...])
    kk = k * k_k
    kk = kk / jnp.maximum(jnp.sqrt(_group_sum(kk * kk, bd_ref)), 1e-12)
    k_ref[...] = k * (1.0 + (a - 1.0) * k_a)
    na_ref[...] = -kk
    bb_ref[...] = kk * a


def _rwkv_pre(h, hp, mix, wr, wk, wv, w1, w2, a1, a2, g1, g2, vecs):
    n = h.shape[0]
    tm = min(256, n)
    assert n % tm == 0
    row = _row_spec(tm, D_MODEL)
    ws, w_specs = zip(*[_weight(w) for w in (wr, wk, wv, w1, w2, a1, a2, g1, g2)])
    return pl.pallas_call(
        _rwkv_pre_body,
        grid=(n // tm,),
        in_specs=[row, row, _const_spec(mix.shape)] + list(w_specs)
        + [_const_spec(vecs.shape), _const_spec((MXU_DIM, MXU_DIM))],
        out_specs=[row] * 7,
        out_shape=[jax.ShapeDtypeStruct((n, D_MODEL), F32)] * 7,
        compiler_params=_params(("parallel",)),
        name="rwkv_pre",
    )(h, hp, mix, *ws, vecs, _block_diag_ones(RWKV_HEAD_DIM))


def _rwkv_scan_body(*refs, c, has_state):
    if has_state:
        r_ref, lw_ref, k_ref, v_ref, na_ref, bb_ref, s0_ref, y_ref, so_ref, s_ref = refs
    else:
        r_ref, lw_ref, k_ref, v_ref, na_ref, bb_ref, y_ref, so_ref, s_ref = refs
    i = pl.program_id(1)
    c2 = 2 * c

    @pl.when(i == 0)
    def _():
        s_ref[...] = s0_ref[...] if has_state else jnp.zeros(s_ref.shape, F32)

    row = lax.broadcasted_iota(jnp.int32, (c, c), 0)
    col = lax.broadcasted_iota(jnp.int32, (c, c), 1)
    tril = (col <= row).astype(BF16)
    row2 = lax.broadcasted_iota(jnp.int32, (c2, c2), 0)
    col2 = lax.broadcasted_iota(jnp.int32, (c2, c2), 1)
    strict = col2 < row2
    incl = col2 <= row2
    eye = (col2 == row2).astype(F32)
    first = lax.broadcasted_iota(jnp.int32, (c, LANES), 1) < RWKV_HEAD_DIM

    def stack(z):
        zero = jnp.zeros_like(z)
        return jnp.concatenate([jnp.where(first, z, zero), jnp.where(first, zero, z)], axis=0)

    pairs = range(RWKV_HEADS // 2)

    def pair(z, p):
        return stack(z[:, p * LANES:(p + 1) * LANES])

    lw = lw_ref[...]
    lw_hi = lw.astype(BF16)
    lw_lo = (lw - lw_hi.astype(F32)).astype(BF16)
    cs = jnp.dot(tril, lw_hi, preferred_element_type=F32) + jnp.dot(tril, lw_lo, preferred_element_type=F32)
    total = cs[c - 1:c, :]
    p_inv = jnp.exp(-cs)
    p_rest = jnp.exp(total - cs)
    decay = jnp.exp(total)
    na_t = na_ref[...] * jnp.exp(cs - lw)
    r_t = r_ref[...] * jnp.exp(cs)
    bb, k, v = bb_ref[...], k_ref[...], v_ref[...]
    bb_t, k_t, bb_h, k_h = bb * p_inv, k * p_inv, bb * p_rest, k * p_rest
    a_bd = [pair(na_t, p).astype(BF16) for p in pairs]
    r_bd = [pair(r_t, p).astype(BF16) for p in pairs]
    b_bd = [pair(bb_t, p).astype(BF16) for p in pairs]
    k_bd = [pair(k_t, p).astype(BF16) for p in pairs]
    v_bd = [pair(v, p) for p in pairs]
    a_ab = [jnp.where(strict, _dot_nt(a_bd[p], b_bd[p]), 0.0) for p in pairs]
    a_ak = [jnp.where(strict, _dot_nt(a_bd[p], k_bd[p]), 0.0) for p in pairs]
    a_rb = [jnp.where(incl, _dot_nt(r_bd[p], b_bd[p]), 0.0) for p in pairs]
    a_rk = [jnp.where(incl, _dot_nt(r_bd[p], k_bd[p]), 0.0) for p in pairs]
    power = a_ab
    inv = [eye + a for a in a_ab]
    step = 2
    while step < c:
        power = [_dot(x, x) for x in power]
        inv = [t + _dot(t, x) for t, x in zip(inv, power)]
        step *= 2
    av = [_dot(a_ak[p], v_bd[p]) for p in pairs]
    w_u0 = [_dot(inv[p], jnp.concatenate([a_bd[p].astype(F32), av[p]], axis=1)) for p in pairs]
    state = [s_ref[p] for p in pairs]
    proj = [_dot_nt(jnp.concatenate([w_u0[p][:, 0:LANES].astype(BF16), r_bd[p]], axis=0), state[p]) for p in pairs]
    u = [proj[p][0:c2, :] + w_u0[p][:, LANES:2 * LANES] for p in pairs]
    y2 = [proj[p][c2:2 * c2, :] + _dot(a_rb[p], u[p]) + _dot(a_rk[p], v_bd[p]) for p in pairs]
    y_ref[...] = jnp.concatenate([y[0:c, :] + y[c:c2, :] for y in y2], axis=1)
    for p in pairs:
        uv = jnp.concatenate([u[p], v_bd[p]], axis=0)
        bk = jnp.concatenate([pair(bb_h, p), pair(k_h, p)], axis=0)
        if 2 * c2 < LANES:
            pad = jnp.zeros((LANES - 2 * c2, LANES), F32)
            uv = jnp.concatenate([uv, pad], axis=0)
            bk = jnp.concatenate([bk, pad], axis=0)
        s_ref[p] = state[p] * decay[:, p * LANES:(p + 1) * LANES] + _dot(uv.T, bk)

    @pl.when(i == pl.num_programs(1) - 1)
    def _():
        so_ref[...] = s_ref[...]


def _rwkv_scan(r, lw, k, v, na, bb, s0, batch, seq):
    c = min(RWKV_CHUNK, seq)
    nc = seq // c
    has_state = s0 is not None
    tok = pl.BlockSpec((c, D_MODEL), lambda b, i: (b * nc + i, 0))
    st = pl.BlockSpec((None, RWKV_HEADS // 2, LANES, LANES), lambda b, i: (b, 0, 0, 0))
    args = [r, lw, k, v, na, bb] + ([s0] if has_state else [])
    return pl.pallas_call(
        functools.partial(_rwkv_scan_body, c=c, has_state=has_state),
        grid=(batch, nc),
        in_specs=[tok] * 6 + ([st] if has_state else []),
        out_specs=[tok, st],
        out_shape=[jax.ShapeDtypeStruct((batch * seq, D_MODEL), F32),
                   jax.ShapeDtypeStruct((batch, RWKV_HEADS // 2, LANES, LANES), F32)],
        scratch_shapes=[pltpu.VMEM((RWKV_HEADS // 2, LANES, LANES), F32)],
        compiler_params=_params(("parallel", "arbitrary")),
        name="rwkv_scan",
    )(*args)


def _pair_states(s):
    b = s.shape[0]
    s = s.reshape(b, RWKV_HEADS // 2, 2, RWKV_HEAD_DIM, RWKV_HEAD_DIM)
    z = jnp.zeros_like(s[:, :, 0])
    top = jnp.concatenate([s[:, :, 0], z], axis=-1)
    bot = jnp.concatenate([z, s[:, :, 1]], axis=-1)
    return jnp.concatenate([top, bot], axis=-2)


def _unpair_states(s):
    b = s.shape[0]
    n = RWKV_HEAD_DIM
    return jnp.stack([s[:, :, :n, :n], s[:, :, n:, n:]], axis=2).reshape(b, RWKV_HEADS, n, n)


def _rwkv_post_body(x_ref, y_ref, r_ref, k_ref, v_ref, g_ref, vec_ref, bd_ref, wo_ref, o_ref):
    y = y_ref[...]
    r_k, ln_w, ln_b = [vec_ref[m:m + 1, :] for m in range(3)]
    inv_n = 1.0 / RWKV_HEAD_DIM
    mu = _group_sum(y, bd_ref, split=True) * inv_n
    yc = y - mu
    var = _group_sum(yc * yc, bd_ref) * inv_n
    yn = yc * lax.rsqrt(var + RWKV_GN_EPS) * ln_w + ln_b
    bonus = _group_sum(r_ref[...] * k_ref[...] * r_k, bd_ref, split=True) * v_ref[...]
    o_ref[...] = x_ref[...] + _dot((yn + bonus) * g_ref[...], wo_ref[...])


def _rwkv_post(x, y, r, k, v, g, vecs, wo):
    n = x.shape[0]
    tm = min(256, n)
    assert n % tm == 0
    row = _row_spec(tm, D_MODEL)
    wo, wo_spec = _weight(wo)
    return pl.pallas_call(
        _rwkv_post_body,
        grid=(n // tm,),
        in_specs=[row] * 6 + [_const_spec(vecs.shape), _const_spec((MXU_DIM, MXU_DIM)), wo_spec],
        out_specs=row,
        out_shape=jax.ShapeDtypeStruct((n, D_MODEL), F32),
        compiler_params=_params(("parallel",)),
        name="rwkv_post",
    )(x, y, r, k, v, g, vecs, _block_diag_ones(RWKV_HEAD_DIM), wo)


def _attn_layer(x, pos, batch, seq, i, w, cache=None):
    lam_init = 0.8 - 0.6 * math.exp(-0.3 * i)
    q, k, kb, v, vt = _attn_proj(x, w["g"], w["wq"], w["wk"], w["wv"], w["qk_norm"], pos)
    if cache is None:
        o = _attn_prompt(q, kb, vt, w["lam"], w["subln"], lam_init, batch, seq)
    else:
        o = _attn_sample(q, k, v, *cache, w["lam"], w["subln"], lam_init, seq)
    shape = (batch, seq, ATT_HEADS, 2 * ATT_HEAD_DIM)
    return _out_proj(x, o, w["wo"]), k.reshape(shape), v.reshape(shape)


def _ret_layer(x, pos, batch, seq, w, s0):
    q, k, v, sg = _ret_proj(x, w["g"], w["wq"], w["wk"], w["wv"], w["wg"], pos)
    og, s = _ret_core(q, k, v, sg, s0, batch, seq)
    return _out_proj(x, og, w["wo"]), s


def _rwkv_layer(x, batch, seq, w, shift, s0):
    h = _norm(x, w["g"])
    h3 = h.reshape(batch, seq, D_MODEL)
    hp = jnp.concatenate([shift[:, None, :], h3[:, :-1]], axis=1).reshape(batch * seq, D_MODEL)
    r, lw, k, v, na, bb, g = _rwkv_pre(h, hp, w["mix"], w["wr"], w["wk"], w["wv"], w["w1"], w["w2"], w["a1"], w["a2"],
                                       w["g1"], w["g2"], w["pre_vecs"])
    y, s = _rwkv_scan(r, lw, k, v, na, bb, None if s0 is None else _pair_states(s0), batch, seq)
    out = _rwkv_post(x, y, r, k, v, g, w["post_vecs"], w["wo"])
    return out, _unpair_states(s), h3[:, -1]


def kernel(x_prompt, x_sample, cache_k, cache_v, page_table, state_ret, state_rwkv, state_rwkv_shift, norm_g, ffn_w_gate, ffn_w_up, ffn_w_down, attn_w_q, attn_w_k, attn_w_v, attn_w_o, attn_qk_norm, attn_lambda, attn_subln, ret_w_q, ret_w_k, ret_w_v, ret_w_g, ret_w_o, rwkv_mix, rwkv_w_r, rwkv_w_k, rwkv_w_v, rwkv_w_o, rwkv_w0, rwkv_w1, rwkv_w2, rwkv_a0, rwkv_a1, rwkv_a2, rwkv_g1, rwkv_g2, rwkv_k_k, rwkv_k_a, rwkv_r_k, rwkv_ln_w, rwkv_ln_b):
    b_p, t_p, d = x_prompt.shape
    b_s, t_s, _ = x_sample.shape
    depth = norm_g.shape[0]
    past = page_table.shape[1] * cache_k.shape[2]
    pos_p = jnp.tile(jnp.arange(t_p, dtype=jnp.float32), b_p)
    pos_s = jnp.tile(past + jnp.arange(t_s, dtype=jnp.float32), b_s)
    yp = x_prompt.reshape(b_p * t_p, d)
    ys = x_sample.reshape(b_s * t_s, d)
    cast = lambda *arrs: [a.astype(BF16) for a in arrs]
    ffn_g, ffn_u, ffn_d = cast(ffn_w_gate, ffn_w_up, ffn_w_down)
    att_q, att_k, att_v, att_o = cast(attn_w_q, attn_w_k, attn_w_v, attn_w_o)
    ret_q, ret_k, ret_v, ret_g, ret_o = cast(ret_w_q, ret_w_k, ret_w_v, ret_w_g, ret_w_o)
    wkv_r, wkv_k, wkv_v, wkv_o = cast(rwkv_w_r, rwkv_w_k, rwkv_w_v, rwkv_w_o)
    wkv_w1, wkv_w2, wkv_a1, wkv_a2, wkv_g1, wkv_g2 = cast(rwkv_w1, rwkv_w2, rwkv_a1, rwkv_a2, rwkv_g1, rwkv_g2)
    outs = {name: [] for name in ("akp", "avp", "aks", "avs", "rsp", "rss", "wsp", "wss", "shp", "shs")}
    for i in range(depth):
        kind, j = i % 3, i // 3
        wg, wu, wd = [[(a, (i, half)) for half in range(2)] for a in (ffn_g, ffn_u, ffn_d)]
        yp = _ffn(yp, norm_g[i, 0], wg[0], wu[0], wd[0])
        ys = _ffn(ys, norm_g[i, 0], wg[0], wu[0], wd[0])
        if kind == 0:
            w = dict(g=norm_g[i, 1], wq=(att_q, (j,)), wk=(att_k, (j,)), wv=(att_v, (j,)), wo=(att_o, (j,)),
                     qk_norm=attn_qk_norm[j], lam=attn_lambda[j], subln=attn_subln[j])
            yp, kp, vp = _attn_layer(yp, pos_p, b_p, t_p, i, w)
            ys, ks, vs = _attn_layer(ys, pos_s, b_s, t_s, i, w, cache=(cache_k, cache_v, page_table, j))
            outs["akp"].append(kp)
            outs["avp"].append(vp)
            outs["aks"].append(ks)
            outs["avs"].append(vs)
        elif kind == 1:
            w = dict(g=norm_g[i, 1], wq=(ret_q, (j,)), wk=(ret_k, (j,)), wv=(ret_v, (j,)), wg=(ret_g, (j,)),
                     wo=(ret_o, (j,)))
            yp, sp = _ret_layer(yp, pos_p, b_p, t_p, w, None)
            ys, ss = _ret_layer(ys, pos_s, b_s, t_s, w, state_ret[j])
            outs["rsp"].append(sp)
            outs["rss"].append(ss)
        else:
            w = dict(g=norm_g[i, 1], mix=rwkv_mix[j], wr=(wkv_r, (j,)), wk=(wkv_k, (j,)), wv=(wkv_v, (j,)),
                     wo=(wkv_o, (j,)), w1=(wkv_w1, (j,)), w2=(wkv_w2, (j,)), a1=(wkv_a1, (j,)), a2=(wkv_a2, (j,)),
                     g1=(wkv_g1, (j,)), g2=(wkv_g2, (j,)),
                     pre_vecs=jnp.stack([rwkv_w0[j], rwkv_a0[j], rwkv_k_k[j], rwkv_k_a[j]]),
                     post_vecs=jnp.stack([rwkv_r_k[j].reshape(d), rwkv_ln_w[j], rwkv_ln_b[j]]))
            yp, sp, lp = _rwkv_layer(yp, b_p, t_p, w, jnp.zeros((b_p, d), F32), None)
            ys, ss, ls = _rwkv_layer(ys, b_s, t_s, w, state_rwkv_shift[j], state_rwkv[j])
            outs["wsp"].append(sp)
            outs["wss"].append(ss)
            outs["shp"].append(lp)
            outs["shs"].append(ls)
        yp = _ffn(yp, norm_g[i, 2], wg[1], wu[1], wd[1])
        ys = _ffn(ys, norm_g[i, 2], wg[1], wu[1], wd[1])
    return (yp.reshape(b_p, t_p, d), ys.reshape(b_s, t_s, d)) + tuple(
        jnp.stack(outs[name]) for name in ("akp", "avp", "aks", "avs", "rsp", "rss", "wsp", "wss", "shp", "shs"))
```

```python
import functools
import math

import jax
import jax.numpy as jnp
from jax import lax
from jax.experimental import pallas as pl
from jax.experimental.pallas import tpu as pltpu

F32 = jnp.float32
BF16 = jnp.bfloat16

D_MODEL = 1024
D_FF = 2816
NORM_EPS = 1e-6
ATT_HEAD_DIM = 64
ATT_HEADS = 8
ATT_SCALE = ATT_HEAD_DIM ** -0.5
Q_SCALE = ATT_SCALE * math.log2(math.e)
ROT_DIM = 16
ROPE_THETA = 500000.0
RET_HEADS = 4
RET_KEY_DIM = 256
RET_VALUE_DIM = 512
RET_CHUNK = 256
RET_THETA = 10000.0
RWKV_HEAD_DIM = 64
RWKV_HEADS = 16
RWKV_GN_EPS = 64e-5
RWKV_CHUNK = 64

LANES = 128
MXU_DIM = 256
VMEM_LIMIT = 56 * 1024 * 1024
TOKEN_TILE = 512
FF_CHUNK = 256


def _params(sem):
    return pltpu.CompilerParams(dimension_semantics=sem, vmem_limit_bytes=VMEM_LIMIT)


def _dot(a, b):
    return jnp.dot(a.astype(BF16), b.astype(BF16), preferred_element_type=F32)


def _dot_nt(a, b):
    return lax.dot_general(a.astype(BF16), b.astype(BF16), (((1,), (1,)), ((), ())),
                           preferred_element_type=F32)


def _rms(x, gain=None, eps=NORM_EPS):
    y = x * lax.rsqrt(jnp.mean(x * x, axis=-1, keepdims=True) + eps)
    return y if gain is None else y * gain


def _const_spec(shape):
    nd = len(shape)
    return pl.BlockSpec(shape, lambda *_: (0,) * nd)


def _row_spec(tm, width):
    return pl.BlockSpec((tm, width), lambda i: (i, 0))


def _weight(w):
    if not isinstance(w, tuple):
        return w, _const_spec(w.shape)
    arr, idx = w
    shape = arr.shape[len(idx):]
    tail = (0,) * len(shape)
    return arr, pl.BlockSpec((None,) * len(idx) + shape, lambda *_: idx + tail)


def _tile(n):
    tm = min(TOKEN_TILE, n)
    assert n % tm == 0
    return tm


def _block_diag_ones(group):
    idx = jnp.arange(MXU_DIM) // group
    return (idx[:, None] == idx[None, :]).astype(BF16)


def _group_sum(x, bd_ref, split=False):
    parts = []
    for c in range(x.shape[1] // MXU_DIM):
        xc = x[:, c * MXU_DIM:(c + 1) * MXU_DIM]
        hi = xc.astype(BF16)
        s = jnp.dot(hi, bd_ref[...], preferred_element_type=F32)
        if split:
            lo = (xc - hi.astype(F32)).astype(BF16)
            s = s + jnp.dot(lo, bd_ref[...], preferred_element_type=F32)
        parts.append(s)
    return jnp.concatenate(parts, axis=1)


def _ffn_body(x_ref, g_ref, wg_ref, wu_ref, wd_ref, o_ref):
    x = x_ref[...]
    h = _rms(x, g_ref[...]).astype(BF16)
    acc = jnp.zeros(x.shape, F32)
    for c in range(D_FF // FF_CHUNK):
        sl = slice(c * FF_CHUNK, (c + 1) * FF_CHUNK)
        gate = jnp.dot(h, wg_ref[:, sl], preferred_element_type=F32)
        up = jnp.dot(h, wu_ref[:, sl], preferred_element_type=F32)
        act = (gate * jax.nn.sigmoid(gate) * up).astype(BF16)
        acc = acc + jnp.dot(act, wd_ref[sl, :], preferred_element_type=F32)
    o_ref[...] = x + 0.5 * acc


def _ffn(x, g, wg, wu, wd):
    n = x.shape[0]
    tm = _tile(n)
    (wg, wg_spec), (wu, wu_spec), (wd, wd_spec) = _weight(wg), _weight(wu), _weight(wd)
    return pl.pallas_call(
        _ffn_body,
        grid=(n // tm,),
        in_specs=[_row_spec(tm, D_MODEL), _const_spec((1, D_MODEL)), wg_spec, wu_spec, wd_spec],
        out_specs=_row_spec(tm, D_MODEL),
        out_shape=jax.ShapeDtypeStruct((n, D_MODEL), F32),
        compiler_params=_params(("parallel",)),
        name="ffn",
    )(x, g.reshape(1, D_MODEL), wg, wu, wd)


def _out_proj_body(x_ref, a_ref, w_ref, o_ref):
    o_ref[...] = x_ref[...] + jnp.dot(a_ref[...].astype(BF16), w_ref[...], preferred_element_type=F32)


def _out_proj(x, a, w):
    n, k = a.shape
    tm = _tile(n)
    w, w_spec = _weight(w)
    return pl.pallas_call(
        _out_proj_body,
        grid=(n // tm,),
        in_specs=[_row_spec(tm, D_MODEL), _row_spec(tm, k), w_spec],
        out_specs=_row_spec(tm, D_MODEL),
        out_shape=jax.ShapeDtypeStruct((n, D_MODEL), F32),
        compiler_params=_params(("parallel",)),
        name="out_proj",
    )(x, a, w)


def _attn_rot_tables(pos):
    half = ROT_DIM // 2
    inv_freq = ROPE_THETA ** (-jnp.arange(half, dtype=jnp.float32) * 2.0 / ROT_DIM)
    ang = pos[:, None] * inv_freq[None, :]
    cos, sin = jnp.cos(ang), jnp.sin(ang)
    n = pos.shape[0]
    ones = jnp.ones((n, ATT_HEAD_DIM - ROT_DIM), jnp.float32)
    zeros = jnp.zeros((n, ATT_HEAD_DIM - ROT_DIM), jnp.float32)
    zh = jnp.zeros((n, half), jnp.float32)
    c = jnp.concatenate([cos, cos, ones], axis=1)
    s_up = jnp.concatenate([-sin, zh, zeros], axis=1)
    s_dn = jnp.concatenate([zh, sin, zeros], axis=1)
    rep = LANES // ATT_HEAD_DIM
    return tuple(jnp.tile(t, (1, rep)) for t in (c, s_up, s_dn))


def _attn_proj_body(x_ref, g_ref, wq_ref, wk_ref, wv_ref, qkn_ref, bd_ref, c_ref, su_ref, sd_ref,
                    q_ref, k_ref, kb_ref, v_ref, vt_ref):
    h = _rms(x_ref[...], g_ref[...]).astype(BF16)
    rep = D_MODEL // LANES
    cos = jnp.concatenate([c_ref[...]] * rep, axis=1)
    s_up = jnp.concatenate([su_ref[...]] * rep, axis=1)
    s_dn = jnp.concatenate([sd_ref[...]] * rep, axis=1)
    half = ROT_DIM // 2

    def norm_rot(z, gain):
        ms = _group_sum(z * z, bd_ref) * (1.0 / ATT_HEAD_DIM)
        z = z * lax.rsqrt(ms + NORM_EPS) * gain
        up = pltpu.roll(z, D_MODEL - half, 1)
        dn = pltpu.roll(z, half, 1)
        return z * cos + up * s_up + dn * s_dn

    q = norm_rot(jnp.dot(h, wq_ref[...], preferred_element_type=F32), qkn_ref[0:1, :])
    q_ref[...] = (q * Q_SCALE).astype(BF16)
    k = norm_rot(jnp.dot(h, wk_ref[...], preferred_element_type=F32), qkn_ref[1:2, :])
    k_ref[...] = k
    kb_ref[...] = k.astype(BF16)
    v = jnp.dot(h, wv_ref[...], preferred_element_type=F32)
    v_ref[...] = v
    vt_ref[...] = v.T.astype(BF16)


def _attn_proj(x, g, wq, wk, wv, qk_norm, pos):
    n = x.shape[0]
    tm = _tile(n)
    qkn = jnp.tile(qk_norm, (1, D_MODEL // ATT_HEAD_DIM))
    tabs = _attn_rot_tables(pos)
    (wq, wq_spec), (wk, wk_spec), (wv, wv_spec) = _weight(wq), _weight(wk), _weight(wv)
    row = _row_spec(tm, D_MODEL)
    tab = _row_spec(tm, LANES)
    f = jax.ShapeDtypeStruct((n, D_MODEL), F32)
    b = jax.ShapeDtypeStruct((n, D_MODEL), BF16)
    return pl.pallas_call(
        _attn_proj_body,
        grid=(n // tm,),
        in_specs=[row, _const_spec((1, D_MODEL)), wq_spec, wk_spec, wv_spec, _const_spec((2, D_MODEL)),
                  _const_spec((MXU_DIM, MXU_DIM)), tab, tab, tab],
        out_specs=[row] * 4 + [pl.BlockSpec((None, D_MODEL, tm), lambda i: (i, 0, 0))],
        out_shape=[b, f, b, f, jax.ShapeDtypeStruct((n // tm, D_MODEL, tm), BF16)],
        compiler_params=_params(("parallel",)),
        name="attn_proj",
    )(x, g.reshape(1, D_MODEL), wq, wk, wv, qkn, _block_diag_ones(ATT_HEAD_DIM), *tabs)


def _half_masks(rows):
    lane = lax.broadcasted_iota(jnp.int32, (rows, LANES), 1)
    return lane < ATT_HEAD_DIM


def _stack_halves(q):
    first = _half_masks(q.shape[0])
    zero = jnp.zeros_like(q)
    return jnp.concatenate([jnp.where(first, q, zero), jnp.where(first, zero, q)], axis=0)


def _softmax_step(s, v, m_ref, l_ref, acc_ref):
    m_prev = m_ref[...]
    m_new = jnp.maximum(m_prev, jnp.max(s, axis=-1, keepdims=True))
    alpha = jnp.exp2(m_prev - m_new)
    p = jnp.exp2(s - m_new[:, 0:1])
    l_ref[...] = alpha * l_ref[...] + jnp.sum(p, axis=-1, keepdims=True)
    acc_ref[...] = alpha * acc_ref[...] + jnp.dot(p.astype(BF16), v, preferred_element_type=F32)
    m_ref[...] = m_new


def _diff_lambda(lam_ref, lam_init):
    lp = lam_ref[...]
    return (jnp.exp(jnp.sum(lp[0:1] * lp[1:2], axis=-1, keepdims=True))
            - jnp.exp(jnp.sum(lp[2:3] * lp[3:4], axis=-1, keepdims=True)) + lam_init)


def _attn_prompt_body(q_ref, k_ref, vt_ref, lam_ref, subln_ref, o_ref, m_ref, l_ref, acc_ref, s_ref, *, tq, lam_init):
    i = pl.program_id(2)
    q2 = _stack_halves(q_ref[...])
    m_ref[...] = jnp.full(m_ref.shape, -jnp.inf, F32)
    l_ref[...] = jnp.zeros(l_ref.shape, F32)
    acc_ref[...] = jnp.zeros(acc_ref.shape, F32)

    def scores(j):
        off = pl.multiple_of(j * tq, tq)
        return _dot_nt(k_ref[pl.ds(off, tq), :], q2)

    def step(j, slot, masked):
        st = s_ref[slot]
        if masked:
            key = lax.broadcasted_iota(jnp.int32, (tq, 2 * tq), 0)
            qry = lax.broadcasted_iota(jnp.int32, (tq, 2 * tq), 1)
            qry = jnp.where(qry >= tq, qry - tq, qry)
            st = jnp.where(key <= qry, st, -jnp.inf)
        m_prev = m_ref[...]
        m_new = jnp.maximum(m_prev, jnp.max(st, axis=0, keepdims=True))
        alpha = jnp.exp2(m_prev - m_new)
        p = jnp.exp2(st - m_new)
        l_ref[...] = alpha * l_ref[...] + jnp.sum(p, axis=0, keepdims=True)
        acc_ref[...] = alpha * acc_ref[...] + jnp.dot(vt_ref[j], p.astype(BF16), preferred_element_type=F32)
        m_ref[...] = m_new

    s_ref[0] = scores(0)

    def two_blocks(jj, carry):
        j = 2 * jj
        s_ref[1] = scores(j + 1)
        step(j, 0, False)
        s_ref[0] = scores(j + 2)
        step(j + 1, 1, False)
        return carry

    lax.fori_loop(0, i // 2, two_blocks, 0)

    @pl.when(i % 2 == 0)
    def _():
        step(i, 0, True)

    @pl.when(i % 2 == 1)
    def _():
        s_ref[1] = scores(i)
        step(i - 1, 0, False)
        step(i, 1, True)

    inv_l = 1.0 / l_ref[...]
    lam = _diff_lambda(lam_ref, lam_init)
    ot = acc_ref[:, 0:tq] * inv_l[:, 0:tq] - lam * (acc_ref[:, tq:2 * tq] * inv_l[:, tq:2 * tq])
    o_ref[...] = _rms(ot.T, subln_ref[...]) * (1.0 - lam_init)


def _attn_prompt(q, k, vt, lam_params, subln, lam_init, batch, seq):
    tq = vt.shape[2]
    nq = seq // tq
    assert seq % tq == 0
    body = functools.partial(_attn_prompt_body, tq=tq, lam_init=lam_init)
    return pl.pallas_call(
        body,
        grid=(batch, ATT_HEADS, nq),
        in_specs=[pl.BlockSpec((tq, LANES), lambda b, h, i: (b * nq + i, h)),
                  pl.BlockSpec((seq, LANES), lambda b, h, i: (b, h)),
                  pl.BlockSpec((nq, LANES, tq), lambda b, h, i: (b, h, 0)),
                  _const_spec((4, ATT_HEAD_DIM)), _const_spec((1, LANES))],
        out_specs=pl.BlockSpec((tq, LANES), lambda b, h, i: (b * nq + i, h)),
        out_shape=jax.ShapeDtypeStruct((batch * seq, D_MODEL), F32),
        scratch_shapes=[pltpu.VMEM((1, 2 * tq), F32), pltpu.VMEM((1, 2 * tq), F32), pltpu.VMEM((LANES, 2 * tq), F32),
                        pltpu.VMEM((2, tq, 2 * tq), F32)],
        compiler_params=_params(("parallel", "parallel", "parallel")),
        name="attn_prompt",
    )(q, k, vt, lam_params, subln.reshape(1, LANES))


def _attn_sample_body(pt_ref, q_ref, kn_ref, vn_ref, *refs, t, group, lam_init):
    del pt_ref
    ck_refs, cv_refs = refs[:group], refs[group:2 * group]
    lam_ref, subln_ref, o_ref, q2_ref, bias_ref, m_ref, l_ref, acc_ref = refs[2 * group:]
    p = pl.program_id(1)
    rows = ATT_HEADS * 2 * t
    head_shift = (2 * t).bit_length() - 1
    tok_shift = ATT_HEADS.bit_length() - 1

    def head_match(shape):
        row = lax.broadcasted_iota(jnp.int32, shape, 0)
        col = lax.broadcasted_iota(jnp.int32, shape, 1)
        return row, col, jnp.bitwise_and(col, ATT_HEADS - 1) == jnp.right_shift(row, head_shift)

    @pl.when(p == 0)
    def _():
        for h in range(ATT_HEADS):
            q2_ref[h * 2 * t:(h + 1) * 2 * t, :] = _stack_halves(q_ref[:, h * LANES:(h + 1) * LANES])
        _, _, ok = head_match(bias_ref.shape)
        bias_ref[...] = jnp.where(ok, 0.0, -jnp.inf)
        m_ref[...] = jnp.full(m_ref.shape, -jnp.inf, F32)
        l_ref[...] = jnp.zeros(l_ref.shape, F32)
        acc_ref[...] = jnp.zeros(acc_ref.shape, F32)

    q2 = q2_ref[...]
    s = jnp.concatenate([_dot_nt(q2, ck[...]) + bias_ref[...] for ck in ck_refs], axis=1)
    m_prev = m_ref[...]
    m_new = jnp.maximum(m_prev, jnp.max(s, axis=-1, keepdims=True))
    alpha = jnp.exp2(m_prev - m_new)
    prob = jnp.exp2(s - m_new[:, 0:1])
    l_ref[...] = alpha * l_ref[...] + jnp.sum(prob, axis=-1, keepdims=True)
    width = bias_ref.shape[1]
    pv = [_dot(prob[:, g * width:(g + 1) * width], cv[...]) for g, cv in enumerate(cv_refs)]
    acc_ref[...] = alpha * acc_ref[...] + functools.reduce(lambda a, b: a + b, pv)
    m_ref[...] = m_new

    @pl.when(p == pl.num_programs(1) - 1)
    def _():
        row, col, ok = head_match((rows, t * ATT_HEADS))
        ok = jnp.logical_and(ok, jnp.right_shift(col, tok_shift) <= jnp.bitwise_and(row, t - 1))
        s_new = jnp.where(ok, _dot_nt(q2_ref[...], kn_ref[...]), -jnp.inf)
        _softmax_step(s_new, vn_ref[...], m_ref, l_ref, acc_ref)
        lam = _diff_lambda(lam_ref, lam_init)
        for h in range(ATT_HEADS):
            lo, mid, hi = h * 2 * t, h * 2 * t + t, (h + 1) * 2 * t
            o = acc_ref[lo:mid, :] / l_ref[lo:mid, :] - lam * (acc_ref[mid:hi, :] / l_ref[mid:hi, :])
            o_ref[:, h * LANES:(h + 1) * LANES] = _rms(o, subln_ref[...]) * (1.0 - lam_init)


def _attn_sample(q, kn, vn, cache_k, cache_v, page_table, layer, lam_params, subln, lam_init, t):
    batch, n_pages = page_table.shape
    page = cache_k.shape[2]
    rows = page * ATT_HEADS
    assert t & (t - 1) == 0 and ATT_HEADS & (ATT_HEADS - 1) == 0
    ck = cache_k.reshape(cache_k.shape[:2] + (rows, LANES))
    cv = cache_v.reshape(cache_v.shape[:2] + (rows, LANES))
    q = q.reshape(batch, t, D_MODEL)
    kn, vn = (z.reshape(batch, t * ATT_HEADS, LANES).astype(BF16) for z in (kn, vn))
    group = max(g for g in (4, 2, 1) if n_pages % g == 0)
    body = functools.partial(_attn_sample_body, t=t, group=group, lam_init=lam_init)
    tok = pl.BlockSpec((None, t, D_MODEL), lambda b, p, pt: (b, 0, 0))
    new = pl.BlockSpec((None, t * ATT_HEADS, LANES), lambda b, p, pt: (b, 0, 0))
    cache = [pl.BlockSpec((None, None, rows, LANES), lambda b, p, pt, g=g: (layer, pt[b, p * group + g], 0, 0))
             for g in range(group)]
    qrows = ATT_HEADS * 2 * t
    grid_spec = pltpu.PrefetchScalarGridSpec(
        num_scalar_prefetch=1,
        grid=(batch, n_pages // group),
        in_specs=[tok, new, new] + cache + cache + [
                  pl.BlockSpec((4, ATT_HEAD_DIM), lambda b, p, pt: (0, 0)),
                  pl.BlockSpec((1, LANES), lambda b, p, pt: (0, 0))],
        out_specs=tok,
        scratch_shapes=[pltpu.VMEM((qrows, LANES), BF16), pltpu.VMEM((qrows, rows), F32)]
        + [pltpu.VMEM((qrows, LANES), F32)] * 3,
    )
    return pl.pallas_call(
        body,
        grid_spec=grid_spec,
        out_shape=jax.ShapeDtypeStruct((batch, t, D_MODEL), F32),
        compiler_params=_params(("parallel", "arbitrary")),
        name="attn_sample",
    )(page_table, q, kn, vn, *([ck] * group), *([cv] * group), lam_params,
      subln.reshape(1, LANES)).reshape(batch * t, D_MODEL)


def _ret_rot_tables(pos):
    half = RET_KEY_DIM // 2
    inv_freq = RET_THETA ** (-jnp.arange(half, dtype=jnp.float32) * 2.0 / RET_KEY_DIM)
    ang = pos[:, None] * inv_freq[None, :]
    return jnp.cos(ang), jnp.sin(ang)


def _ret_proj_body(x_ref, g_ref, wq_ref, wk_ref, wv_ref, wg_ref, c_ref, s_ref, q_ref, k_ref, v_ref, sg_ref):
    h = _rms(x_ref[...], g_ref[...]).astype(BF16)
    cos, sin = c_ref[...], s_ref[...]

    def rot(z):
        parts = []
        for hd in range(RET_HEADS):
            x1 = z[:, hd * RET_KEY_DIM:hd * RET_KEY_DIM + LANES]
            x2 = z[:, hd * RET_KEY_DIM + LANES:(hd + 1) * RET_KEY_DIM]
            parts += [x1 * cos - x2 * sin, x2 * cos + x1 * sin]
        return jnp.concatenate(parts, axis=1)

    q_ref[...] = rot(jnp.dot(h, wq_ref[...], preferred_element_type=F32))
    k_ref[...] = rot(jnp.dot(h, wk_ref[...], preferred_element_type=F32) * (RET_KEY_DIM ** -0.5))
    v_ref[...] = jnp.dot(h, wv_ref[...], preferred_element_type=F32)
    gate = jnp.dot(h, wg_ref[...], preferred_element_type=F32)
    sg_ref[...] = gate * jax.nn.sigmoid(gate)


def _ret_proj(x, g, wq, wk, wv, wg, pos):
    n = x.shape[0]
    tm = _tile(n)
    cos, sin = _ret_rot_tables(pos)
    vw = RET_HEADS * RET_VALUE_DIM
    row = _row_spec(tm, D_MODEL)
    wide = _row_spec(tm, vw)
    tab = _row_spec(tm, LANES)
    ws = [_weight(w) for w in (wq, wk, wv, wg)]
    return pl.pallas_call(
        _ret_proj_body,
        grid=(n // tm,),
        in_specs=[row, _const_spec((1, D_MODEL))] + [spec for _, spec in ws] + [tab, tab],
        out_specs=[row, row, wide, wide],
        out_shape=[jax.ShapeDtypeStruct((n, D_MODEL), F32), jax.ShapeDtypeStruct((n, D_MODEL), F32),
                   jax.ShapeDtypeStruct((n, vw), F32), jax.ShapeDtypeStruct((n, vw), F32)],
        compiler_params=_params(("parallel",)),
        name="ret_proj",
    )(x, g.reshape(1, D_MODEL), *[w for w, _ in ws], cos, sin)


def _ret_decay_tables(c):
    lg = jnp.log(1.0 - 2.0 ** (-5.0 - jnp.arange(RET_HEADS, dtype=jnp.float32)))
    idx = jnp.arange(c, dtype=jnp.float32)
    diff = idx[:, None] - idx[None, :]
    d_intra = jnp.where(diff[None] >= 0.0, jnp.exp(jnp.maximum(diff, 0.0)[None] * lg[:, None, None]), 0.0)
    q_dec = jnp.exp((idx[:, None] + 1.0) * lg[None, :])
    k_dec = jnp.exp((c - 1.0 - idx)[:, None] * lg[None, :])
    c_dec = jnp.exp(c * lg)
    return d_intra, q_dec.T[:, :, None], k_dec.T[:, :, None], c_dec[:, None, None]


def _ret_core_body(*refs, c, has_state):
    if has_state:
        q_ref, k_ref, v_ref, sg_ref, d_ref, qd_ref, kd_ref, cd_ref, s0_ref, o_ref, so_ref, s_ref = refs
    else:
        q_ref, k_ref, v_ref, sg_ref, d_ref, qd_ref, kd_ref, cd_ref, o_ref, so_ref, s_ref = refs
    i = pl.program_id(1)

    @pl.when(i == 0)
    def _():
        s_ref[...] = s0_ref[...] if has_state else jnp.zeros(s_ref.shape, F32)

    for h in range(RET_HEADS):
        ks = slice(h * RET_KEY_DIM, (h + 1) * RET_KEY_DIM)
        vs = slice(h * RET_VALUE_DIM, (h + 1) * RET_VALUE_DIM)
        q, k, v = q_ref[:, ks], k_ref[:, ks], v_ref[:, vs]
        state = s_ref[h]
        att = _dot_nt(q, k) * d_ref[h]
        o = _dot(att, v) + _dot(q * qd_ref[h], state)
        kd = k * kd_ref[h]
        if c < LANES:
            kd = jnp.concatenate([kd, jnp.zeros((LANES - c, kd.shape[1]), F32)], axis=0)
            v = jnp.concatenate([v, jnp.zeros((LANES - c, v.shape[1]), v.dtype)], axis=0)
        s_ref[h] = state * cd_ref[h] + _dot(kd.T, v)
        o_ref[:, vs] = _rms(o) * sg_ref[:, vs]

    @pl.when(i == pl.num_programs(1) - 1)
    def _():
        so_ref[...] = s_ref[...]


def _ret_core(q, k, v, sg, s0, batch, seq):
    c = min(RET_CHUNK, seq)
    nc = seq // c
    has_state = s0 is not None
    tables = _ret_decay_tables(c)
    tok = lambda w: pl.BlockSpec((c, w), lambda b, i: (b * nc + i, 0))
    st = pl.BlockSpec((None, RET_HEADS, RET_KEY_DIM, RET_VALUE_DIM), lambda b, i: (b, 0, 0, 0))
    kw, vw = RET_HEADS * RET_KEY_DIM, RET_HEADS * RET_VALUE_DIM
    in_specs = [tok(kw), tok(kw), tok(vw), tok(vw)] + [_const_spec(t.shape) for t in tables]
    args = [q, k, v, sg, *tables]
    if has_state:
        in_specs.append(st)
        args.append(s0)
    return pl.pallas_call(
        functools.partial(_ret_core_body, c=c, has_state=has_state),
        grid=(batch, nc),
        in_specs=in_specs,
        out_specs=[tok(vw), st],
        out_shape=[jax.ShapeDtypeStruct((batch * seq, vw), F32),
                   jax.ShapeDtypeStruct((batch, RET_HEADS, RET_KEY_DIM, RET_VALUE_DIM), F32)],
        scratch_shapes=[pltpu.VMEM((RET_HEADS, RET_KEY_DIM, RET_VALUE_DIM), F32)],
        compiler_params=_params(("parallel", "arbitrary")),
        name="ret_core",
    )(*args)


def _norm_body(x_ref, g_ref, o_ref):
    o_ref[...] = _rms(x_ref[...], g_ref[...])


def _norm(x, g):
    n = x.shape[0]
    tm = _tile(n)
    return pl.pallas_call(
        _norm_body,
        grid=(n // tm,),
        in_specs=[_row_spec(tm, D_MODEL), _const_spec((1, D_MODEL))],
        out_specs=_row_spec(tm, D_MODEL),
        out_shape=jax.ShapeDtypeStruct((n, D_MODEL), F32),
        compiler_params=_params(("parallel",)),
        name="norm",
    )(x, g.reshape(1, D_MODEL))


def _rwkv_pre_body(x_ref, seed_ref, *refs, tm, seq):
    long_seq = seq >= tm
    if long_seq:
        xp_ref, refs = refs[0], refs[1:]
    (gn_ref, mix_ref, wr_ref, wk_ref, wv_ref, w1_ref, w2_ref, a1_ref, a2_ref, g1_ref, g2_ref,
     vec_ref, bd_ref, r_ref, lw_ref, k_ref, v_ref, na_ref, bb_ref, g_ref) = refs
    h = _rms(x_ref[...], gn_ref[...])
    rolled = pltpu.roll(h, 1, 0)
    row = lax.broadcasted_iota(jnp.int32, h.shape, 0)
    if long_seq:
        before = _rms(xp_ref[7:8, :], gn_ref[...])
        starts = pl.program_id(0) % (seq // tm) == 0
        first = jnp.where(starts, seed_ref[...], before)
        hp = jnp.where(row == 0, first, rolled)
    else:
        hp = jnp.where(jnp.bitwise_and(row, seq - 1) == 0, seed_ref[...], rolled)
    xx = hp - h
    xr, xw, xk, xv, xa, xg = [h + xx * mix_ref[m:m + 1, :] for m in range(6)]
    w0, a0, k_k, k_a = [vec_ref[m:m + 1, :] for m in range(4)]
    r_ref[...] = _dot(xr, wr_ref[...])
    w_log = -jax.nn.softplus(-(w0 + _dot(jnp.tanh(_dot(xw, w1_ref[...])), w2_ref[...]))) - 0.5
    lw_ref[...] = -jnp.exp(w_log)
    k = _dot(xk, wk_ref[...])
    v_ref[...] = _dot(xv, wv_ref[...])
    a = jax.nn.sigmoid(a0 + _dot(_dot(xa, a1_ref[...]), a2_ref[...]))
    g_ref[...] = _dot(jax.nn.sigmoid(_dot(xg, g1_ref[...])), g2_ref[...])
    kk = k * k_k
    kk = kk / jnp.maximum(jnp.sqrt(_group_sum(kk * kk, bd_ref)), 1e-12)
    k_ref[...] = k * (1.0 + (a - 1.0) * k_a)
    na_ref[...] = -kk
    bb_ref[...] = kk * a


def _rwkv_pre(x, g, shift, seq, mix, wr, wk, wv, w1, w2, a1, a2, g1, g2, vecs):
    n = x.shape[0]
    tm = min(256, n)
    assert n % tm == 0 and (seq % tm == 0 or (tm % seq == 0 and seq & (seq - 1) == 0))
    row = _row_spec(tm, D_MODEL)
    ws, w_specs = zip(*[_weight(w) for w in (wr, wk, wv, w1, w2, a1, a2, g1, g2)])
    if seq >= tm:
        per_seq = seq // tm
        sub = tm // 8
        seed = [shift[:, None, :], x]
        seed_specs = [pl.BlockSpec((None, 1, D_MODEL), lambda i: (i // per_seq, 0, 0)),
                      pl.BlockSpec((8, D_MODEL), lambda i: (jnp.maximum(i * sub - 1, 0), 0))]
    else:
        seed = [jnp.repeat(shift, seq, axis=0)]
        seed_specs = [row]
    return pl.pallas_call(
        functools.partial(_rwkv_pre_body, tm=tm, seq=seq),
        grid=(n // tm,),
        in_specs=[row] + seed_specs + [_const_spec((1, D_MODEL)), _const_spec(mix.shape)] + list(w_specs)
        + [_const_spec(vecs.shape), _const_spec((MXU_DIM, MXU_DIM))],
        out_specs=[row] * 7,
        out_shape=[jax.ShapeDtypeStruct((n, D_MODEL), F32)] * 7,
        compiler_params=_params(("parallel",)),
        name="rwkv_pre",
    )(x, *seed, g.reshape(1, D_MODEL), mix, *ws, vecs, _block_diag_ones(RWKV_HEAD_DIM))


def _rwkv_scan_body(*refs, c, has_state):
    if has_state:
        r_ref, lw_ref, k_ref, v_ref, na_ref, bb_ref, s0_ref, y_ref, so_ref, s_ref = refs
    else:
        r_ref, lw_ref, k_ref, v_ref, na_ref, bb_ref, y_ref, so_ref, s_ref = refs
    i = pl.program_id(1)
    c2 = 2 * c

    n = RWKV_HEAD_DIM

    @pl.when(i == 0)
    def _():
        if has_state:
            z = jnp.zeros((n, n), F32)
            for p in range(RWKV_HEADS // 2):
                top = jnp.concatenate([s0_ref[2 * p], z], axis=1)
                bot = jnp.concatenate([z, s0_ref[2 * p + 1]], axis=1)
                s_ref[p] = jnp.concatenate([top, bot], axis=0)
        else:
            s_ref[...] = jnp.zeros(s_ref.shape, F32)

    row = lax.broadcasted_iota(jnp.int32, (c, c), 0)
    col = lax.broadcasted_iota(jnp.int32, (c, c), 1)
    tril = (col <= row).astype(BF16)
    row2 = lax.broadcasted_iota(jnp.int32, (c2, c2), 0)
    col2 = lax.broadcasted_iota(jnp.int32, (c2, c2), 1)
    strict = col2 < row2
    incl = col2 <= row2
    eye = (col2 == row2).astype(F32)
    first = lax.broadcasted_iota(jnp.int32, (c, LANES), 1) < RWKV_HEAD_DIM

    def stack(z):
        zero = jnp.zeros_like(z)
        return jnp.concatenate([jnp.where(first, z, zero), jnp.where(first, zero, z)], axis=0)

    pairs = range(RWKV_HEADS // 2)

    def pair(z, p):
        return stack(z[:, p * LANES:(p + 1) * LANES])

    lw = lw_ref[...]
    lw_hi = lw.astype(BF16)
    lw_lo = (lw - lw_hi.astype(F32)).astype(BF16)
    cs = jnp.dot(tril, lw_hi, preferred_element_type=F32) + jnp.dot(tril, lw_lo, preferred_element_type=F32)
    total = cs[c - 1:c, :]
    p_inv = jnp.exp(-cs)
    p_rest = jnp.exp(total - cs)
    decay = jnp.exp(total)
    na_t = na_ref[...] * jnp.exp(cs - lw)
    r_t = r_ref[...] * jnp.exp(cs)
    bb, k, v = bb_ref[...], k_ref[...], v_ref[...]
    bb_t, k_t, bb_h, k_h = bb * p_inv, k * p_inv, bb * p_rest, k * p_rest
    a_bd = [pair(na_t, p).astype(BF16) for p in pairs]
    r_bd = [pair(r_t, p).astype(BF16) for p in pairs]
    b_bd = [pair(bb_t, p).astype(BF16) for p in pairs]
    k_bd = [pair(k_t, p).astype(BF16) for p in pairs]
    v_bd = [pair(v, p) for p in pairs]
    wide = c2 % LANES == 0
    if wide:
        a_all = [_dot_nt(jnp.concatenate([a_bd[p], r_bd[p]], axis=0), jnp.concatenate([b_bd[p], k_bd[p]], axis=0))
                 for p in pairs]
        blocks = [[a[0:c2, 0:c2], a[0:c2, c2:2 * c2], a[c2:2 * c2, 0:c2], a[c2:2 * c2, c2:2 * c2]] for a in a_all]
    else:
        blocks = [[_dot_nt(a_bd[p], b_bd[p]), _dot_nt(a_bd[p], k_bd[p]), _dot_nt(r_bd[p], b_bd[p]),
                   _dot_nt(r_bd[p], k_bd[p])] for p in pairs]
    a_ab = [jnp.where(strict, b[0], 0.0) for b in blocks]
    a_ak = [jnp.where(strict, b[1], 0.0) for b in blocks]
    a_rb = [jnp.where(incl, b[2], 0.0) for b in blocks]
    a_rk = [jnp.where(incl, b[3], 0.0) for b in blocks]
    inv = [eye + a for a in a_ab]
    if wide:
        power = [_dot(x, x) for x in a_ab]
        step = 4
        while step < c:
            both = [_dot(x, jnp.concatenate([x, t], axis=1)) for x, t in zip(power, inv)]
            inv = [t + z[:, c2:2 * c2] for t, z in zip(inv, both)]
            power = [z[:, 0:c2] for z in both]
            step *= 2
        inv = [t + _dot(x, t) for t, x in zip(inv, power)]
    else:
        power = a_ab
        step = 2
        while step < c:
            power = [_dot(x, x) for x in power]
            inv = [t + _dot(t, x) for t, x in zip(inv, power)]
            step *= 2
    av = [_dot(a_ak[p], v_bd[p]) for p in pairs]
    w_u0 = [_dot(inv[p], jnp.concatenate([a_bd[p].astype(F32), av[p]], axis=1)) for p in pairs]
    state = [s_ref[p] for p in pairs]
    proj = [_dot_nt(jnp.concatenate([w_u0[p][:, 0:LANES].astype(BF16), r_bd[p]], axis=0), state[p]) for p in pairs]
    u = [proj[p][0:c2, :] + w_u0[p][:, LANES:2 * LANES] for p in pairs]
    if wide:
        y2 = [proj[p][c2:2 * c2, :] + _dot(jnp.concatenate([a_rb[p], a_rk[p]], axis=1),
                                           jnp.concatenate([u[p], v_bd[p]], axis=0)) for p in pairs]
    else:
        y2 = [proj[p][c2:2 * c2, :] + _dot(a_rb[p], u[p]) + _dot(a_rk[p], v_bd[p]) for p in pairs]
    y_ref[...] = jnp.concatenate([y[0:c, :] + y[c:c2, :] for y in y2], axis=1)
    for p in pairs:
        uv = jnp.concatenate([u[p], v_bd[p]], axis=0)
        bk = jnp.concatenate([pair(bb_h, p), pair(k_h, p)], axis=0)
        if 2 * c2 < LANES:
            pad = jnp.zeros((LANES - 2 * c2, LANES), F32)
            uv = jnp.concatenate([uv, pad], axis=0)
            bk = jnp.concatenate([bk, pad], axis=0)
        s_ref[p] = state[p] * decay[:, p * LANES:(p + 1) * LANES] + _dot(uv.T, bk)

    @pl.when(i == pl.num_programs(1) - 1)
    def _():
        for p in range(RWKV_HEADS // 2):
            s = s_ref[p]
            so_ref[2 * p] = s[0:n, 0:n]
            so_ref[2 * p + 1] = s[n:2 * n, n:2 * n]


def _rwkv_scan(r, lw, k, v, na, bb, s0, batch, seq):
    c = min(RWKV_CHUNK, seq)
    nc = seq // c
    has_state = s0 is not None
    tok = pl.BlockSpec((c, D_MODEL), lambda b, i: (b * nc + i, 0))
    st = pl.BlockSpec((None, RWKV_HEADS, RWKV_HEAD_DIM, RWKV_HEAD_DIM), lambda b, i: (b, 0, 0, 0))
    args = [r, lw, k, v, na, bb] + ([s0] if has_state else [])
    return pl.pallas_call(
        functools.partial(_rwkv_scan_body, c=c, has_state=has_state),
        grid=(batch, nc),
        in_specs=[tok] * 6 + ([st] if has_state else []),
        out_specs=[tok, st],
        out_shape=[jax.ShapeDtypeStruct((batch * seq, D_MODEL), F32),
                   jax.ShapeDtypeStruct((batch, RWKV_HEADS, RWKV_HEAD_DIM, RWKV_HEAD_DIM), F32)],
        scratch_shapes=[pltpu.VMEM((RWKV_HEADS // 2, LANES, LANES), F32)],
        compiler_params=_params(("parallel", "arbitrary")),
        name="rwkv_scan",
    )(*args)


def _rwkv_post_body(x_ref, y_ref, r_ref, k_ref, v_ref, g_ref, vec_ref, bd_ref, wo_ref, o_ref):
    y = y_ref[...]
    r_k, ln_w, ln_b = [vec_ref[m:m + 1, :] for m in range(3)]
    inv_n = 1.0 / RWKV_HEAD_DIM
    mu = _group_sum(y, bd_ref, split=True) * inv_n
    yc = y - mu
    var = _group_sum(yc * yc, bd_ref) * inv_n
    yn = yc * lax.rsqrt(var + RWKV_GN_EPS) * ln_w + ln_b
    bonus = _group_sum(r_ref[...] * k_ref[...] * r_k, bd_ref, split=True) * v_ref[...]
    o_ref[...] = x_ref[...] + _dot((yn + bonus) * g_ref[...], wo_ref[...])


def _rwkv_post(x, y, r, k, v, g, vecs, wo):
    n = x.shape[0]
    tm = min(256, n)
    assert n % tm == 0
    row = _row_spec(tm, D_MODEL)
    wo, wo_spec = _weight(wo)
    return pl.pallas_call(
        _rwkv_post_body,
        grid=(n // tm,),
        in_specs=[row] * 6 + [_const_spec(vecs.shape), _const_spec((MXU_DIM, MXU_DIM)), wo_spec],
        out_specs=row,
        out_shape=jax.ShapeDtypeStruct((n, D_MODEL), F32),
        compiler_params=_params(("parallel",)),
        name="rwkv_post",
    )(x, y, r, k, v, g, vecs, _block_diag_ones(RWKV_HEAD_DIM), wo)


def _attn_layer(x, pos, batch, seq, i, w, cache=None):
    lam_init = 0.8 - 0.6 * math.exp(-0.3 * i)
    q, k, kb, v, vt = _attn_proj(x, w["g"], w["wq"], w["wk"], w["wv"], w["qk_norm"], pos)
    if cache is None:
        o = _attn_prompt(q, kb, vt, w["lam"], w["subln"], lam_init, batch, seq)
    else:
        o = _attn_sample(q, k, v, *cache, w["lam"], w["subln"], lam_init, seq)
    shape = (batch, seq, ATT_HEADS, 2 * ATT_HEAD_DIM)
    return _out_proj(x, o, w["wo"]), k.reshape(shape), v.reshape(shape)


def _ret_layer(x, pos, batch, seq, w, s0):
    q, k, v, sg = _ret_proj(x, w["g"], w["wq"], w["wk"], w["wv"], w["wg"], pos)
    og, s = _ret_core(q, k, v, sg, s0, batch, seq)
    return _out_proj(x, og, w["wo"]), s


def _rwkv_layer(x, batch, seq, w, shift, s0):
    last = _norm(x.reshape(batch, seq, D_MODEL)[:, -1], w["g"])
    r, lw, k, v, na, bb, g = _rwkv_pre(x, w["g"], shift, seq, w["mix"], w["wr"], w["wk"], w["wv"], w["w1"], w["w2"],
                                       w["a1"], w["a2"], w["g1"], w["g2"], w["pre_vecs"])
    y, s = _rwkv_scan(r, lw, k, v, na, bb, s0, batch, seq)
    out = _rwkv_post(x, y, r, k, v, g, w["post_vecs"], w["wo"])
    return out, s, last


def kernel(x_prompt, x_sample, cache_k, cache_v, page_table, state_ret, state_rwkv, state_rwkv_shift, norm_g, ffn_w_gate, ffn_w_up, ffn_w_down, attn_w_q, attn_w_k, attn_w_v, attn_w_o, attn_qk_norm, attn_lambda, attn_subln, ret_w_q, ret_w_k, ret_w_v, ret_w_g, ret_w_o, rwkv_mix, rwkv_w_r, rwkv_w_k, rwkv_w_v, rwkv_w_o, rwkv_w0, rwkv_w1, rwkv_w2, rwkv_a0, rwkv_a1, rwkv_a2, rwkv_g1, rwkv_g2, rwkv_k_k, rwkv_k_a, rwkv_r_k, rwkv_ln_w, rwkv_ln_b):
    b_p, t_p, d = x_prompt.shape
    b_s, t_s, _ = x_sample.shape
    depth = norm_g.shape[0]
    past = page_table.shape[1] * cache_k.shape[2]
    pos_p = jnp.tile(jnp.arange(t_p, dtype=jnp.float32), b_p)
    pos_s = jnp.tile(past + jnp.arange(t_s, dtype=jnp.float32), b_s)
    yp = x_prompt.reshape(b_p * t_p, d)
    ys = x_sample.reshape(b_s * t_s, d)
    cast = lambda *arrs: [a.astype(BF16) for a in arrs]
    ffn_g, ffn_u, ffn_d = cast(ffn_w_gate, ffn_w_up, ffn_w_down)
    att_q, att_k, att_v, att_o = cast(attn_w_q, attn_w_k, attn_w_v, attn_w_o)
    ret_q, ret_k, ret_v, ret_g, ret_o = cast(ret_w_q, ret_w_k, ret_w_v, ret_w_g, ret_w_o)
    wkv_r, wkv_k, wkv_v, wkv_o = cast(rwkv_w_r, rwkv_w_k, rwkv_w_v, rwkv_w_o)
    wkv_w1, wkv_w2, wkv_a1, wkv_a2, wkv_g1, wkv_g2 = cast(rwkv_w1, rwkv_w2, rwkv_a1, rwkv_a2, rwkv_g1, rwkv_g2)
    outs = {name: [] for name in ("akp", "avp", "aks", "avs", "rsp", "rss", "wsp", "wss", "shp", "shs")}
    for i in range(depth):
        kind, j = i % 3, i // 3
        wg, wu, wd = [[(a, (i, half)) for half in range(2)] for a in (ffn_g, ffn_u, ffn_d)]
        yp = _ffn(yp, norm_g[i, 0], wg[0], wu[0], wd[0])
        ys = _ffn(ys, norm_g[i, 0], wg[0], wu[0], wd[0])
        if kind == 0:
            w = dict(g=norm_g[i, 1], wq=(att_q, (j,)), wk=(att_k, (j,)), wv=(att_v, (j,)), wo=(att_o, (j,)),
                     qk_norm=attn_qk_norm[j], lam=attn_lambda[j], subln=attn_subln[j])
            yp, kp, vp = _attn_layer(yp, pos_p, b_p, t_p, i, w)
            ys, ks, vs = _attn_layer(ys, pos_s, b_s, t_s, i, w, cache=(cache_k, cache_v, page_table, j))
            outs["akp"].append(kp)
            outs["avp"].append(vp)
            outs["aks"].append(ks)
            outs["avs"].append(vs)
        elif kind == 1:
            w = dict(g=norm_g[i, 1], wq=(ret_q, (j,)), wk=(ret_k, (j,)), wv=(ret_v, (j,)), wg=(ret_g, (j,)),
                     wo=(ret_o, (j,)))
            yp, sp = _ret_layer(yp, pos_p, b_p, t_p, w, None)
            ys, ss = _ret_layer(ys, pos_s, b_s, t_s, w, state_ret[j])
            outs["rsp"].append(sp)
            outs["rss"].append(ss)
        else:
            w = dict(g=norm_g[i, 1], mix=rwkv_mix[j], wr=(wkv_r, (j,)), wk=(wkv_k, (j,)), wv=(wkv_v, (j,)),
                     wo=(wkv_o, (j,)), w1=(wkv_w1, (j,)), w2=(wkv_w2, (j,)), a1=(wkv_a1, (j,)), a2=(wkv_a2, (j,)),
                     g1=(wkv_g1, (j,)), g2=(wkv_g2, (j,)),
                     pre_vecs=jnp.stack([rwkv_w0[j], rwkv_a0[j], rwkv_k_k[j], rwkv_k_a[j]]),
                     post_vecs=jnp.stack([rwkv_r_k[j].reshape(d), rwkv_ln_w[j], rwkv_ln_b[j]]))
            yp, sp, lp = _rwkv_layer(yp, b_p, t_p, w, jnp.zeros((b_p, d), F32), None)
            ys, ss, ls = _rwkv_layer(ys, b_s, t_s, w, state_rwkv_shift[j], state_rwkv[j])
            outs["wsp"].append(sp)
            outs["wss"].append(ss)
            outs["shp"].append(lp)
            outs["shs"].append(ls)
        yp = _ffn(yp, norm_g[i, 2], wg[1], wu[1], wd[1])
        ys = _ffn(ys, norm_g[i, 2], wg[1], wu[1], wd[1])
    return (yp.reshape(b_p, t_p, d), ys.reshape(b_s, t_s, d)) + tuple(
        jnp.stack(outs[name]) for name in ("akp", "avp", "aks", "avs", "rsp", "rss", "wsp", "wss", "shp", "shs"))
```

```python
import functools
import math

import jax
import jax.numpy as jnp
from jax import lax
from jax.experimental import pallas as pl
from jax.experimental.pallas import tpu as pltpu

F32 = jnp.float32
BF16 = jnp.bfloat16

D_MODEL = 1024
D_FF = 2816
NORM_EPS = 1e-6
ATT_HEAD_DIM = 64
ATT_HEADS = 8
ATT_SCALE = ATT_HEAD_DIM ** -0.5
Q_SCALE = ATT_SCALE * math.log2(math.e)
ROT_DIM = 16
ROPE_THETA = 500000.0
RET_HEADS = 4
RET_KEY_DIM = 256
RET_VALUE_DIM = 512
RET_CHUNK = 256
RET_THETA = 10000.0
RWKV_HEAD_DIM = 64
RWKV_HEADS = 16
RWKV_GN_EPS = 64e-5
RWKV_CHUNK = 64

LANES = 128
MXU_DIM = 256
VMEM_LIMIT = 56 * 1024 * 1024
TOKEN_TILE = 512
FF_CHUNK = 256
PAGE_LOOKAHEAD = 2


def _params(sem):
    return pltpu.CompilerParams(dimension_semantics=sem, vmem_limit_bytes=VMEM_LIMIT)


def _dot(a, b):
    return jnp.dot(a.astype(BF16), b.astype(BF16), preferred_element_type=F32)


def _dot_nt(a, b):
    return lax.dot_general(a.astype(BF16), b.astype(BF16), (((1,), (1,)), ((), ())),
                           preferred_element_type=F32)


def _rms(x, gain=None, eps=NORM_EPS):
    y = x * lax.rsqrt(jnp.mean(x * x, axis=-1, keepdims=True) + eps)
    return y if gain is None else y * gain


def _const_spec(shape):
    nd = len(shape)
    return pl.BlockSpec(shape, lambda *_: (0,) * nd)


def _row_spec(tm, width):
    return pl.BlockSpec((tm, width), lambda i: (i, 0))


def _weight(w, single=False):
    arr, idx = w if isinstance(w, tuple) else (w, ())
    shape = arr.shape[len(idx):]
    tail = (0,) * len(shape)
    mode = pl.Buffered(1) if single else None
    return arr, pl.BlockSpec((None,) * len(idx) + shape, lambda *_: idx + tail, pipeline_mode=mode)


def _tile(n):
    tm = min(TOKEN_TILE, n)
    assert n % tm == 0
    return tm


def _block_diag_ones(group):
    idx = jnp.arange(MXU_DIM) // group
    return (idx[:, None] == idx[None, :]).astype(BF16)


def _group_sum(x, bd_ref, split=False):
    parts = []
    for c in range(x.shape[1] // MXU_DIM):
        xc = x[:, c * MXU_DIM:(c + 1) * MXU_DIM]
        hi = xc.astype(BF16)
        s = jnp.dot(hi, bd_ref[...], preferred_element_type=F32)
        if split:
            lo = (xc - hi.astype(F32)).astype(BF16)
            s = s + jnp.dot(lo, bd_ref[...], preferred_element_type=F32)
        parts.append(s)
    return jnp.concatenate(parts, axis=1)


def _ffn_body(x_ref, *refs, fused):
    x = x_ref[...]
    if fused:
        a_ref, wo_ref, refs = refs[0], refs[1], refs[2:]
        x = x + jnp.dot(a_ref[...].astype(BF16), wo_ref[...], preferred_element_type=F32)
    g_ref, wg_ref, wu_ref, wd_ref, o_ref = refs
    h = _rms(x, g_ref[...]).astype(BF16)
    acc = jnp.zeros(x.shape, F32)
    for c in range(D_FF // FF_CHUNK):
        sl = slice(c * FF_CHUNK, (c + 1) * FF_CHUNK)
        gate = jnp.dot(h, wg_ref[:, sl], preferred_element_type=F32)
        up = jnp.dot(h, wu_ref[:, sl], preferred_element_type=F32)
        act = (gate * jax.nn.sigmoid(gate) * up).astype(BF16)
        acc = acc + jnp.dot(act, wd_ref[sl, :], preferred_element_type=F32)
    o_ref[...] = x + 0.5 * acc


def _ffn(x, g, wg, wu, wd, proj=None):
    n = x.shape[0]
    tm = _tile(n)
    fused = proj is not None
    (wg, wg_spec), (wu, wu_spec), (wd, wd_spec) = [_weight(w, single=fused) for w in (wg, wu, wd)]
    extra, extra_specs = [], []
    if fused:
        a, (wo, wo_spec) = proj[0], _weight(proj[1], single=True)
        extra, extra_specs = [a, wo], [_row_spec(tm, a.shape[1]), wo_spec]
    return pl.pallas_call(
        functools.partial(_ffn_body, fused=fused),
        grid=(n // tm,),
        in_specs=[_row_spec(tm, D_MODEL)] + extra_specs + [_const_spec((1, D_MODEL)), wg_spec, wu_spec, wd_spec],
        out_specs=_row_spec(tm, D_MODEL),
        out_shape=jax.ShapeDtypeStruct((n, D_MODEL), F32),
        compiler_params=_params(("parallel",)),
        name="ffn_proj" if fused else "ffn",
    )(x, *extra, g.reshape(1, D_MODEL), wg, wu, wd)


def _attn_rot_tables(pos):
    half = ROT_DIM // 2
    inv_freq = ROPE_THETA ** (-jnp.arange(half, dtype=jnp.float32) * 2.0 / ROT_DIM)
    ang = pos[:, None] * inv_freq[None, :]
    cos, sin = jnp.cos(ang), jnp.sin(ang)
    n = pos.shape[0]
    ones = jnp.ones((n, ATT_HEAD_DIM - ROT_DIM), jnp.float32)
    zeros = jnp.zeros((n, ATT_HEAD_DIM - ROT_DIM), jnp.float32)
    zh = jnp.zeros((n, half), jnp.float32)
    c = jnp.concatenate([cos, cos, ones], axis=1)
    s_up = jnp.concatenate([-sin, zh, zeros], axis=1)
    s_dn = jnp.concatenate([zh, sin, zeros], axis=1)
    rep = LANES // ATT_HEAD_DIM
    return tuple(jnp.tile(t, (1, rep)) for t in (c, s_up, s_dn))


def _attn_proj_body(x_ref, g_ref, wq_ref, wk_ref, wv_ref, qkn_ref, bd_ref, c_ref, su_ref, sd_ref,
                    q_ref, k_ref, kb_ref, v_ref, vt_ref):
    h = _rms(x_ref[...], g_ref[...]).astype(BF16)
    rep = D_MODEL // LANES
    cos = jnp.concatenate([c_ref[...]] * rep, axis=1)
    s_up = jnp.concatenate([su_ref[...]] * rep, axis=1)
    s_dn = jnp.concatenate([sd_ref[...]] * rep, axis=1)
    half = ROT_DIM // 2

    def norm_rot(z, gain):
        ms = _group_sum(z * z, bd_ref) * (1.0 / ATT_HEAD_DIM)
        z = z * lax.rsqrt(ms + NORM_EPS) * gain
        up = pltpu.roll(z, D_MODEL - half, 1)
        dn = pltpu.roll(z, half, 1)
        return z * cos + up * s_up + dn * s_dn

    q = norm_rot(jnp.dot(h, wq_ref[...], preferred_element_type=F32), qkn_ref[0:1, :])
    q_ref[...] = (q * Q_SCALE).astype(BF16)
    k = norm_rot(jnp.dot(h, wk_ref[...], preferred_element_type=F32), qkn_ref[1:2, :])
    k_ref[...] = k
    kb_ref[...] = k.astype(BF16)
    v = jnp.dot(h, wv_ref[...], preferred_element_type=F32)
    v_ref[...] = v
    vt_ref[...] = v.T.astype(BF16)


def _attn_proj(x, g, wq, wk, wv, qk_norm, pos):
    n = x.shape[0]
    tm = _tile(n)
    qkn = jnp.tile(qk_norm, (1, D_MODEL // ATT_HEAD_DIM))
    tabs = _attn_rot_tables(pos)
    (wq, wq_spec), (wk, wk_spec), (wv, wv_spec) = _weight(wq), _weight(wk), _weight(wv)
    row = _row_spec(tm, D_MODEL)
    tab = _row_spec(tm, LANES)
    f = jax.ShapeDtypeStruct((n, D_MODEL), F32)
    b = jax.ShapeDtypeStruct((n, D_MODEL), BF16)
    return pl.pallas_call(
        _attn_proj_body,
        grid=(n // tm,),
        in_specs=[row, _const_spec((1, D_MODEL)), wq_spec, wk_spec, wv_spec, _const_spec((2, D_MODEL)),
                  _const_spec((MXU_DIM, MXU_DIM)), tab, tab, tab],
        out_specs=[row] * 4 + [pl.BlockSpec((None, D_MODEL, tm), lambda i: (i, 0, 0))],
        out_shape=[b, f, b, f, jax.ShapeDtypeStruct((n // tm, D_MODEL, tm), BF16)],
        compiler_params=_params(("parallel",)),
        name="attn_proj",
    )(x, g.reshape(1, D_MODEL), wq, wk, wv, qkn, _block_diag_ones(ATT_HEAD_DIM), *tabs)


def _half_masks(rows):
    lane = lax.broadcasted_iota(jnp.int32, (rows, LANES), 1)
    return lane < ATT_HEAD_DIM


def _stack_halves(q):
    first = _half_masks(q.shape[0])
    zero = jnp.zeros_like(q)
    return jnp.concatenate([jnp.where(first, q, zero), jnp.where(first, zero, q)], axis=0)


def _softmax_step(s, v, m_ref, l_ref, acc_ref):
    m_prev = m_ref[...]
    m_new = jnp.maximum(m_prev, jnp.max(s, axis=-1, keepdims=True))
    alpha = jnp.exp2(m_prev - m_new)
    p = jnp.exp2(s - m_new[:, 0:1])
    l_ref[...] = alpha * l_ref[...] + jnp.sum(p, axis=-1, keepdims=True)
    acc_ref[...] = alpha * acc_ref[...] + jnp.dot(p.astype(BF16), v, preferred_element_type=F32)
    m_ref[...] = m_new


def _diff_lambda(lam_ref, lam_init):
    lp = lam_ref[...]
    return (jnp.exp(jnp.sum(lp[0:1] * lp[1:2], axis=-1, keepdims=True))
            - jnp.exp(jnp.sum(lp[2:3] * lp[3:4], axis=-1, keepdims=True)) + lam_init)


def _attn_prompt_body(q_ref, k_ref, vt_ref, lam_ref, subln_ref, o_ref, m_ref, l_ref, acc_ref, s_ref, *, tq, lam_init):
    i = pl.program_id(2)
    q2 = _stack_halves(q_ref[...])
    m_ref[...] = jnp.full(m_ref.shape, -jnp.inf, F32)
    l_ref[...] = jnp.zeros(l_ref.shape, F32)
    acc_ref[...] = jnp.zeros(acc_ref.shape, F32)

    def scores(j):
        off = pl.multiple_of(j * tq, tq)
        return _dot_nt(k_ref[pl.ds(off, tq), :], q2)

    def step(j, slot, masked):
        st = s_ref[slot]
        if masked:
            key = lax.broadcasted_iota(jnp.int32, (tq, 2 * tq), 0)
            qry = lax.broadcasted_iota(jnp.int32, (tq, 2 * tq), 1)
            qry = jnp.where(qry >= tq, qry - tq, qry)
            st = jnp.where(key <= qry, st, -jnp.inf)
        m_prev = m_ref[...]
        m_new = jnp.maximum(m_prev, jnp.max(st, axis=0, keepdims=True))
        alpha = jnp.exp2(m_prev - m_new)
        p = jnp.exp2(st - m_new)
        l_ref[...] = alpha * l_ref[...] + jnp.sum(p, axis=0, keepdims=True)
        acc_ref[...] = alpha * acc_ref[...] + jnp.dot(vt_ref[j], p.astype(BF16), preferred_element_type=F32)
        m_ref[...] = m_new

    s_ref[0] = scores(0)

    def two_blocks(jj, carry):
        j = 2 * jj
        s_ref[1] = scores(j + 1)
        step(j, 0, False)
        s_ref[0] = scores(j + 2)
        step(j + 1, 1, False)
        return carry

    lax.fori_loop(0, i // 2, two_blocks, 0)

    @pl.when(i % 2 == 0)
    def _():
        step(i, 0, True)

    @pl.when(i % 2 == 1)
    def _():
        s_ref[1] = scores(i)
        step(i - 1, 0, False)
        step(i, 1, True)

    inv_l = 1.0 / l_ref[...]
    lam = _diff_lambda(lam_ref, lam_init)
    ot = acc_ref[:, 0:tq] * inv_l[:, 0:tq] - lam * (acc_ref[:, tq:2 * tq] * inv_l[:, tq:2 * tq])
    o_ref[...] = _rms(ot.T, subln_ref[...]) * (1.0 - lam_init)


def _attn_prompt(q, k, vt, lam_params, subln, lam_init, batch, seq):
    tq = vt.shape[2]
    nq = seq // tq
    assert seq % tq == 0
    body = functools.partial(_attn_prompt_body, tq=tq, lam_init=lam_init)
    return pl.pallas_call(
        body,
        grid=(batch, ATT_HEADS, nq),
        in_specs=[pl.BlockSpec((tq, LANES), lambda b, h, i: (b * nq + i, h)),
                  pl.BlockSpec((seq, LANES), lambda b, h, i: (b, h)),
                  pl.BlockSpec((nq, LANES, tq), lambda b, h, i: (b, h, 0)),
                  _const_spec((4, ATT_HEAD_DIM)), _const_spec((1, LANES))],
        out_specs=pl.BlockSpec((tq, LANES), lambda b, h, i: (b * nq + i, h)),
        out_shape=jax.ShapeDtypeStruct((batch * seq, D_MODEL), F32),
        scratch_shapes=[pltpu.VMEM((1, 2 * tq), F32), pltpu.VMEM((1, 2 * tq), F32), pltpu.VMEM((LANES, 2 * tq), F32),
                        pltpu.VMEM((2, tq, 2 * tq), F32)],
        compiler_params=_params(("parallel", "parallel", "parallel")),
        name="attn_prompt",
    )(q, k, vt, lam_params, subln.reshape(1, LANES))


def _attn_sample_body(pt_ref, q_ref, kn_ref, vn_ref, ck_hbm, cv_hbm, lam_ref, subln_ref, o_ref,
                      q2_ref, bias_ref, m_ref, l_ref, acc_ref, kbuf, vbuf, sem, *, t, group, layer, steps, total,
                      lam_init):
    p = pl.program_id(1)
    g = pl.program_id(0) * steps + p
    n_slots = PAGE_LOOKAHEAD + 1
    slot = lax.rem(g, n_slots)

    def copies(step, sl):
        b_idx, p_idx = lax.div(step, steps), lax.rem(step, steps)
        out = []
        for j in range(group):
            page = pt_ref[b_idx, p_idx * group + j]
            out.append(pltpu.make_async_copy(ck_hbm.at[layer, page], kbuf.at[sl, j], sem.at[0, sl, j]))
            out.append(pltpu.make_async_copy(cv_hbm.at[layer, page], vbuf.at[sl, j], sem.at[1, sl, j]))
        return out

    def start(step, sl):
        for n, cp in enumerate(copies(step, sl)):
            cp.start(priority=n // 2 % 2)

    @pl.when(g == 0)
    def _():
        for ahead in range(min(PAGE_LOOKAHEAD, total)):
            start(g + ahead, ahead)

    @pl.when(g + PAGE_LOOKAHEAD < total)
    def _():
        start(g + PAGE_LOOKAHEAD, lax.rem(g + PAGE_LOOKAHEAD, n_slots))

    for cp in copies(g, slot):
        cp.wait()
    ck_refs = [kbuf.at[slot, j] for j in range(group)]
    cv_refs = [vbuf.at[slot, j] for j in range(group)]
    rows = ATT_HEADS * 2 * t
    head_shift = (2 * t).bit_length() - 1
    tok_shift = ATT_HEADS.bit_length() - 1

    def head_match(shape):
        row = lax.broadcasted_iota(jnp.int32, shape, 0)
        col = lax.broadcasted_iota(jnp.int32, shape, 1)
        return row, col, jnp.bitwise_and(col, ATT_HEADS - 1) == jnp.right_shift(row, head_shift)

    @pl.when(p == 0)
    def _():
        for h in range(ATT_HEADS):
            q2_ref[h * 2 * t:(h + 1) * 2 * t, :] = _stack_halves(q_ref[:, h * LANES:(h + 1) * LANES])
        _, _, ok = head_match(bias_ref.shape)
        bias_ref[...] = jnp.where(ok, 0.0, -jnp.inf)
        m_ref[...] = jnp.full(m_ref.shape, -jnp.inf, F32)
        l_ref[...] = jnp.zeros(l_ref.shape, F32)
        acc_ref[...] = jnp.zeros(acc_ref.shape, F32)

    q2 = q2_ref[...]
    scores = [_dot_nt(q2, ck[...]) + bias_ref[...] for ck in ck_refs]
    m, l, acc = m_ref[...], l_ref[...], acc_ref[...]
    for s, cv in zip(scores, cv_refs):
        m_new = jnp.maximum(m, jnp.max(s, axis=-1, keepdims=True))
        alpha = jnp.exp2(m - m_new)
        prob = jnp.exp2(s - m_new[:, 0:1])
        l = alpha * l + jnp.sum(prob, axis=-1, keepdims=True)
        acc = alpha * acc + _dot(prob, cv[...])
        m = m_new
    m_ref[...], l_ref[...], acc_ref[...] = m, l, acc

    @pl.when(p == pl.num_programs(1) - 1)
    def _():
        row, col, ok = head_match((rows, t * ATT_HEADS))
        ok = jnp.logical_and(ok, jnp.right_shift(col, tok_shift) <= jnp.bitwise_and(row, t - 1))
        s_new = jnp.where(ok, _dot_nt(q2_ref[...], kn_ref[...]), -jnp.inf)
        _softmax_step(s_new, vn_ref[...], m_ref, l_ref, acc_ref)
        lam = _diff_lambda(lam_ref, lam_init)
        for h in range(ATT_HEADS):
            lo, mid, hi = h * 2 * t, h * 2 * t + t, (h + 1) * 2 * t
            o = acc_ref[lo:mid, :] / l_ref[lo:mid, :] - lam * (acc_ref[mid:hi, :] / l_ref[mid:hi, :])
            o_ref[:, h * LANES:(h + 1) * LANES] = _rms(o, subln_ref[...]) * (1.0 - lam_init)


def _attn_sample(q, kn, vn, cache_k, cache_v, page_table, layer, lam_params, subln, lam_init, t):
    batch, n_pages = page_table.shape
    page = cache_k.shape[2]
    rows = page * ATT_HEADS
    assert t & (t - 1) == 0 and ATT_HEADS & (ATT_HEADS - 1) == 0
    ck = cache_k.reshape(cache_k.shape[:2] + (rows, LANES))
    cv = cache_v.reshape(cache_v.shape[:2] + (rows, LANES))
    q = q.reshape(batch, t, D_MODEL)
    kn, vn = (z.reshape(batch, t * ATT_HEADS, LANES).astype(BF16) for z in (kn, vn))
    group = max(g for g in (4, 2, 1) if n_pages % g == 0)
    steps = n_pages // group
    body = functools.partial(_attn_sample_body, t=t, group=group, layer=layer, steps=steps, total=batch * steps,
                             lam_init=lam_init)
    tok = pl.BlockSpec((None, t, D_MODEL), lambda b, p, pt: (b, 0, 0))
    new = pl.BlockSpec((None, t * ATT_HEADS, LANES), lambda b, p, pt: (b, 0, 0))
    cache = pl.BlockSpec(memory_space=pl.ANY)
    qrows = ATT_HEADS * 2 * t
    grid_spec = pltpu.PrefetchScalarGridSpec(
        num_scalar_prefetch=1,
        grid=(batch, steps),
        in_specs=[tok, new, new, cache, cache,
                  pl.BlockSpec((4, ATT_HEAD_DIM), lambda b, p, pt: (0, 0)),
                  pl.BlockSpec((1, LANES), lambda b, p, pt: (0, 0))],
        out_specs=tok,
        scratch_shapes=[pltpu.VMEM((qrows, LANES), BF16), pltpu.VMEM((qrows, rows), F32)]
        + [pltpu.VMEM((qrows, LANES), F32)] * 3
        + [pltpu.VMEM((PAGE_LOOKAHEAD + 1, group, rows, LANES), F32)] * 2
        + [pltpu.SemaphoreType.DMA((2, PAGE_LOOKAHEAD + 1, group))],
    )
    return pl.pallas_call(
        body,
        grid_spec=grid_spec,
        out_shape=jax.ShapeDtypeStruct((batch, t, D_MODEL), F32),
        compiler_params=_params(("arbitrary", "arbitrary")),
        name="attn_sample",
    )(page_table, q, kn, vn, ck, cv, lam_params, subln.reshape(1, LANES)).reshape(batch * t, D_MODEL)


def _ret_rot_tables(pos):
    half = RET_KEY_DIM // 2
    inv_freq = RET_THETA ** (-jnp.arange(half, dtype=jnp.float32) * 2.0 / RET_KEY_DIM)
    ang = pos[:, None] * inv_freq[None, :]
    return jnp.cos(ang), jnp.sin(ang)


def _ret_proj_body(x_ref, g_ref, wq_ref, wk_ref, wv_ref, wg_ref, c_ref, s_ref, q_ref, k_ref, v_ref, sg_ref):
    h = _rms(x_ref[...], g_ref[...]).astype(BF16)
    cos, sin = c_ref[...], s_ref[...]

    def rot(z):
        parts = []
        for hd in range(RET_HEADS):
            x1 = z[:, hd * RET_KEY_DIM:hd * RET_KEY_DIM + LANES]
            x2 = z[:, hd * RET_KEY_DIM + LANES:(hd + 1) * RET_KEY_DIM]
            parts += [x1 * cos - x2 * sin, x2 * cos + x1 * sin]
        return jnp.concatenate(parts, axis=1)

    q_ref[...] = rot(jnp.dot(h, wq_ref[...], preferred_element_type=F32))
    k_ref[...] = rot(jnp.dot(h, wk_ref[...], preferred_element_type=F32) * (RET_KEY_DIM ** -0.5))
    v_ref[...] = jnp.dot(h, wv_ref[...], preferred_element_type=F32)
    gate = jnp.dot(h, wg_ref[...], preferred_element_type=F32)
    sg_ref[...] = gate * jax.nn.sigmoid(gate)


def _ret_proj(x, g, wq, wk, wv, wg, pos):
    n = x.shape[0]
    tm = _tile(n)
    cos, sin = _ret_rot_tables(pos)
    vw = RET_HEADS * RET_VALUE_DIM
    row = _row_spec(tm, D_MODEL)
    wide = _row_spec(tm, vw)
    tab = _row_spec(tm, LANES)
    ws = [_weight(w) for w in (wq, wk, wv, wg)]
    return pl.pallas_call(
        _ret_proj_body,
        grid=(n // tm,),
        in_specs=[row, _const_spec((1, D_MODEL))] + [spec for _, spec in ws] + [tab, tab],
        out_specs=[row, row, wide, wide],
        out_shape=[jax.ShapeDtypeStruct((n, D_MODEL), F32), jax.ShapeDtypeStruct((n, D_MODEL), F32),
                   jax.ShapeDtypeStruct((n, vw), F32), jax.ShapeDtypeStruct((n, vw), F32)],
        compiler_params=_params(("parallel",)),
        name="ret_proj",
    )(x, g.reshape(1, D_MODEL), *[w for w, _ in ws], cos, sin)


def _ret_decay_tables(c):
    lg = jnp.log(1.0 - 2.0 ** (-5.0 - jnp.arange(RET_HEADS, dtype=jnp.float32)))
    idx = jnp.arange(c, dtype=jnp.float32)
    diff = idx[:, None] - idx[None, :]
    d_intra = jnp.where(diff[None] >= 0.0, jnp.exp(jnp.maximum(diff, 0.0)[None] * lg[:, None, None]), 0.0)
    q_dec = jnp.exp((idx[:, None] + 1.0) * lg[None, :])
    k_dec = jnp.exp((c - 1.0 - idx)[:, None] * lg[None, :])
    c_dec = jnp.exp(c * lg)
    return d_intra, q_dec.T[:, :, None], k_dec.T[:, :, None], c_dec[:, None, None]


def _ret_core_body(*refs, c, has_state):
    if has_state:
        q_ref, k_ref, v_ref, sg_ref, d_ref, qd_ref, kd_ref, cd_ref, s0_ref, o_ref, so_ref, s_ref = refs
    else:
        q_ref, k_ref, v_ref, sg_ref, d_ref, qd_ref, kd_ref, cd_ref, o_ref, so_ref, s_ref = refs
    i = pl.program_id(1)

    @pl.when(i == 0)
    def _():
        s_ref[...] = s0_ref[...] if has_state else jnp.zeros(s_ref.shape, F32)

    for h in range(RET_HEADS):
        ks = slice(h * RET_KEY_DIM, (h + 1) * RET_KEY_DIM)
        vs = slice(h * RET_VALUE_DIM, (h + 1) * RET_VALUE_DIM)
        q, k, v = q_ref[:, ks], k_ref[:, ks], v_ref[:, vs]
        state = s_ref[h]
        att = _dot_nt(q, k) * d_ref[h]
        o = _dot(att, v) + _dot(q * qd_ref[h], state)
        kd = k * kd_ref[h]
        if c < LANES:
            kd = jnp.concatenate([kd, jnp.zeros((LANES - c, kd.shape[1]), F32)], axis=0)
            v = jnp.concatenate([v, jnp.zeros((LANES - c, v.shape[1]), v.dtype)], axis=0)
        s_ref[h] = state * cd_ref[h] + _dot(kd.T, v)
        o_ref[:, vs] = _rms(o) * sg_ref[:, vs]

    @pl.when(i == pl.num_programs(1) - 1)
    def _():
        so_ref[...] = s_ref[...]


def _ret_core(q, k, v, sg, s0, batch, seq):
    c = min(RET_CHUNK, seq)
    nc = seq // c
    has_state = s0 is not None
    tables = _ret_decay_tables(c)
    tok = lambda w: pl.BlockSpec((c, w), lambda b, i: (b * nc + i, 0))
    st = pl.BlockSpec((None, RET_HEADS, RET_KEY_DIM, RET_VALUE_DIM), lambda b, i: (b, 0, 0, 0))
    kw, vw = RET_HEADS * RET_KEY_DIM, RET_HEADS * RET_VALUE_DIM
    in_specs = [tok(kw), tok(kw), tok(vw), tok(vw)] + [_const_spec(t.shape) for t in tables]
    args = [q, k, v, sg, *tables]
    if has_state:
        in_specs.append(st)
        args.append(s0)
    return pl.pallas_call(
        functools.partial(_ret_core_body, c=c, has_state=has_state),
        grid=(batch, nc),
        in_specs=in_specs,
        out_specs=[tok(vw), st],
        out_shape=[jax.ShapeDtypeStruct((batch * seq, vw), F32),
                   jax.ShapeDtypeStruct((batch, RET_HEADS, RET_KEY_DIM, RET_VALUE_DIM), F32)],
        scratch_shapes=[pltpu.VMEM((RET_HEADS, RET_KEY_DIM, RET_VALUE_DIM), F32)],
        compiler_params=_params(("parallel", "arbitrary")),
        name="ret_core",
    )(*args)


def _norm_body(x_ref, g_ref, o_ref):
    o_ref[...] = _rms(x_ref[...], g_ref[...])


def _norm(x, g):
    n = x.shape[0]
    tm = _tile(n)
    return pl.pallas_call(
        _norm_body,
        grid=(n // tm,),
        in_specs=[_row_spec(tm, D_MODEL), _const_spec((1, D_MODEL))],
        out_specs=_row_spec(tm, D_MODEL),
        out_shape=jax.ShapeDtypeStruct((n, D_MODEL), F32),
        compiler_params=_params(("parallel",)),
        name="norm",
    )(x, g.reshape(1, D_MODEL))


def _rwkv_pre_body(x_ref, seed_ref, *refs, tm, seq):
    long_seq = seq >= tm
    if long_seq:
        xp_ref, refs = refs[0], refs[1:]
    (gn_ref, mix_ref, wr_ref, wk_ref, wv_ref, w1_ref, w2_ref, a1_ref, a2_ref, g1_ref, g2_ref,
     vec_ref, bd_ref, r_ref, lw_ref, k_ref, v_ref, na_ref, bb_ref, g_ref) = refs
    h = _rms(x_ref[...], gn_ref[...])
    rolled = pltpu.roll(h, 1, 0)
    row = lax.broadcasted_iota(jnp.int32, h.shape, 0)
    if long_seq:
        before = _rms(xp_ref[7:8, :], gn_ref[...])
        starts = pl.program_id(0) % (seq // tm) == 0
        first = jnp.where(starts, seed_ref[...], before)
        hp = jnp.where(row == 0, first, rolled)
    else:
        hp = jnp.where(jnp.bitwise_and(row, seq - 1) == 0, seed_ref[...], rolled)
    xx = hp - h
    xr, xw, xk, xv, xa, xg = [h + xx * mix_ref[m:m + 1, :] for m in range(6)]
    w0, a0, k_k, k_a = [vec_ref[m:m + 1, :] for m in range(4)]
    r_ref[...] = _dot(xr, wr_ref[...])
    w_log = -jax.nn.softplus(-(w0 + _dot(jnp.tanh(_dot(xw, w1_ref[...])), w2_ref[...]))) - 0.5
    lw_ref[...] = -jnp.exp(w_log)
    k = _dot(xk, wk_ref[...])
    v_ref[...] = _dot(xv, wv_ref[...])
    a = jax.nn.sigmoid(a0 + _dot(_dot(xa, a1_ref[...]), a2_ref[...]))
    g_ref[...] = _dot(jax.nn.sigmoid(_dot(xg, g1_ref[...])), g2_ref[...])
    kk = k * k_k
    kk = kk / jnp.maximum(jnp.sqrt(_group_sum(kk * kk, bd_ref)), 1e-12)
    k_ref[...] = k * (1.0 + (a - 1.0) * k_a)
    na_ref[...] = -kk
    bb_ref[...] = kk * a


def _rwkv_pre(x, g, shift, seq, mix, wr, wk, wv, w1, w2, a1, a2, g1, g2, vecs):
    n = x.shape[0]
    tm = min(256, n)
    assert n % tm == 0 and (seq % tm == 0 or (tm % seq == 0 and seq & (seq - 1) == 0))
    row = _row_spec(tm, D_MODEL)
    ws, w_specs = zip(*[_weight(w) for w in (wr, wk, wv, w1, w2, a1, a2, g1, g2)])
    if seq >= tm:
        per_seq = seq // tm
        sub = tm // 8
        seed = [shift[:, None, :], x]
        seed_specs = [pl.BlockSpec((None, 1, D_MODEL), lambda i: (i // per_seq, 0, 0)),
                      pl.BlockSpec((8, D_MODEL), lambda i: (jnp.maximum(i * sub - 1, 0), 0))]
    else:
        seed = [jnp.repeat(shift, seq, axis=0)]
        seed_specs = [row]
    return pl.pallas_call(
        functools.partial(_rwkv_pre_body, tm=tm, seq=seq),
        grid=(n // tm,),
        in_specs=[row] + seed_specs + [_const_spec((1, D_MODEL)), _const_spec(mix.shape)] + list(w_specs)
        + [_const_spec(vecs.shape), _const_spec((MXU_DIM, MXU_DIM))],
        out_specs=[row] * 7,
        out_shape=[jax.ShapeDtypeStruct((n, D_MODEL), F32)] * 7,
        compiler_params=_params(("parallel",)),
        name="rwkv_pre",
    )(x, *seed, g.reshape(1, D_MODEL), mix, *ws, vecs, _block_diag_ones(RWKV_HEAD_DIM))


def _rwkv_scan_body(*refs, c, has_state):
    if has_state:
        r_ref, lw_ref, k_ref, v_ref, na_ref, bb_ref, s0_ref, y_ref, so_ref, s_ref = refs
    else:
        r_ref, lw_ref, k_ref, v_ref, na_ref, bb_ref, y_ref, so_ref, s_ref = refs
    i = pl.program_id(1)
    c2 = 2 * c

    n = RWKV_HEAD_DIM

    @pl.when(i == 0)
    def _():
        if has_state:
            z = jnp.zeros((n, n), F32)
            for p in range(RWKV_HEADS // 2):
                top = jnp.concatenate([s0_ref[2 * p], z], axis=1)
                bot = jnp.concatenate([z, s0_ref[2 * p + 1]], axis=1)
                s_ref[p] = jnp.concatenate([top, bot], axis=0)
        else:
            s_ref[...] = jnp.zeros(s_ref.shape, F32)

    row = lax.broadcasted_iota(jnp.int32, (c, c), 0)
    col = lax.broadcasted_iota(jnp.int32, (c, c), 1)
    tril = (col <= row).astype(BF16)
    row2 = lax.broadcasted_iota(jnp.int32, (c2, c2), 0)
    col2 = lax.broadcasted_iota(jnp.int32, (c2, c2), 1)
    strict = col2 < row2
    incl = col2 <= row2
    eye = (col2 == row2).astype(F32)
    first = lax.broadcasted_iota(jnp.int32, (c, LANES), 1) < RWKV_HEAD_DIM

    def stack(z):
        zero = jnp.zeros_like(z)
        return jnp.concatenate([jnp.where(first, z, zero), jnp.where(first, zero, z)], axis=0)

    pairs = range(RWKV_HEADS // 2)

    def pair(z, p):
        return stack(z[:, p * LANES:(p + 1) * LANES])

    lw = lw_ref[...]
    lw_hi = lw.astype(BF16)
    lw_lo = (lw - lw_hi.astype(F32)).astype(BF16)
    cs = jnp.dot(tril, lw_hi, preferred_element_type=F32) + jnp.dot(tril, lw_lo, preferred_element_type=F32)
    total = cs[c - 1:c, :]
    p_inv = jnp.exp(-cs)
    p_rest = jnp.exp(total - cs)
    decay = jnp.exp(total)
    na_t = na_ref[...] * jnp.exp(cs - lw)
    r_t = r_ref[...] * jnp.exp(cs)
    bb, k, v = bb_ref[...], k_ref[...], v_ref[...]
    bb_t, k_t, bb_h, k_h = bb * p_inv, k * p_inv, bb * p_rest, k * p_rest
    a_bd = [pair(na_t, p).astype(BF16) for p in pairs]
    r_bd = [pair(r_t, p).astype(BF16) for p in pairs]
    b_bd = [pair(bb_t, p).astype(BF16) for p in pairs]
    k_bd = [pair(k_t, p).astype(BF16) for p in pairs]
    v_bd = [pair(v, p) for p in pairs]
    wide = c2 % LANES == 0
    if wide:
        a_all = [_dot_nt(jnp.concatenate([a_bd[p], r_bd[p]], axis=0), jnp.concatenate([b_bd[p], k_bd[p]], axis=0))
                 for p in pairs]
        blocks = [[a[0:c2, 0:c2], a[0:c2, c2:2 * c2], a[c2:2 * c2, 0:c2], a[c2:2 * c2, c2:2 * c2]] for a in a_all]
    else:
        blocks = [[_dot_nt(a_bd[p], b_bd[p]), _dot_nt(a_bd[p], k_bd[p]), _dot_nt(r_bd[p], b_bd[p]),
                   _dot_nt(r_bd[p], k_bd[p])] for p in pairs]
    a_ab = [jnp.where(strict, b[0], 0.0) for b in blocks]
    a_ak = [jnp.where(strict, b[1], 0.0) for b in blocks]
    a_rb = [jnp.where(incl, b[2], 0.0) for b in blocks]
    a_rk = [jnp.where(incl, b[3], 0.0) for b in blocks]
    inv = [eye + a for a in a_ab]
    if wide:
        power = [_dot(x, x) for x in a_ab]
        step = 4
        while step < c:
            both = [_dot(x, jnp.concatenate([x, t], axis=1)) for x, t in zip(power, inv)]
            inv = [t + z[:, c2:2 * c2] for t, z in zip(inv, both)]
            power = [z[:, 0:c2] for z in both]
            step *= 2
        inv = [t + _dot(x, t) for t, x in zip(inv, power)]
    else:
        power = a_ab
        step = 2
        while step < c:
            power = [_dot(x, x) for x in power]
            inv = [t + _dot(t, x) for t, x in zip(inv, power)]
            step *= 2
    av = [_dot(a_ak[p], v_bd[p]) for p in pairs]
    w_u0 = [_dot(inv[p], jnp.concatenate([a_bd[p].astype(F32), av[p]], axis=1)) for p in pairs]
    state = [s_ref[p] for p in pairs]
    proj = [_dot_nt(jnp.concatenate([w_u0[p][:, 0:LANES].astype(BF16), r_bd[p]], axis=0), state[p]) for p in pairs]
    u = [proj[p][0:c2, :] + w_u0[p][:, LANES:2 * LANES] for p in pairs]
    if wide:
        y2 = [proj[p][c2:2 * c2, :] + _dot(jnp.concatenate([a_rb[p], a_rk[p]], axis=1),
                                           jnp.concatenate([u[p], v_bd[p]], axis=0)) for p in pairs]
    else:
        y2 = [proj[p][c2:2 * c2, :] + _dot(a_rb[p], u[p]) + _dot(a_rk[p], v_bd[p]) for p in pairs]
    y_ref[...] = jnp.concatenate([y[0:c, :] + y[c:c2, :] for y in y2], axis=1)
    for p in pairs:
        uv = jnp.concatenate([u[p], v_bd[p]], axis=0)
        bk = jnp.concatenate([pair(bb_h, p), pair(k_h, p)], axis=0)
        if 2 * c2 < LANES:
            pad = jnp.zeros((LANES - 2 * c2, LANES), F32)
            uv = jnp.concatenate([uv, pad], axis=0)
            bk = jnp.concatenate([bk, pad], axis=0)
        s_ref[p] = state[p] * decay[:, p * LANES:(p + 1) * LANES] + _dot(uv.T, bk)

    @pl.when(i == pl.num_programs(1) - 1)
    def _():
        for p in range(RWKV_HEADS // 2):
            s = s_ref[p]
            so_ref[2 * p] = s[0:n, 0:n]
            so_ref[2 * p + 1] = s[n:2 * n, n:2 * n]


def _rwkv_scan(r, lw, k, v, na, bb, s0, batch, seq):
    c = min(RWKV_CHUNK, seq)
    nc = seq // c
    has_state = s0 is not None
    tok = pl.BlockSpec((c, D_MODEL), lambda b, i: (b * nc + i, 0))
    st = pl.BlockSpec((None, RWKV_HEADS, RWKV_HEAD_DIM, RWKV_HEAD_DIM), lambda b, i: (b, 0, 0, 0))
    args = [r, lw, k, v, na, bb] + ([s0] if has_state else [])
    return pl.pallas_call(
        functools.partial(_rwkv_scan_body, c=c, has_state=has_state),
        grid=(batch, nc),
        in_specs=[tok] * 6 + ([st] if has_state else []),
        out_specs=[tok, st],
        out_shape=[jax.ShapeDtypeStruct((batch * seq, D_MODEL), F32),
                   jax.ShapeDtypeStruct((batch, RWKV_HEADS, RWKV_HEAD_DIM, RWKV_HEAD_DIM), F32)],
        scratch_shapes=[pltpu.VMEM((RWKV_HEADS // 2, LANES, LANES), F32)],
        compiler_params=_params(("parallel", "arbitrary")),
        name="rwkv_scan",
    )(*args)


def _rwkv_post_body(x_ref, y_ref, r_ref, k_ref, v_ref, g_ref, vec_ref, bd_ref, wo_ref, o_ref):
    y = y_ref[...]
    r_k, ln_w, ln_b = [vec_ref[m:m + 1, :] for m in range(3)]
    inv_n = 1.0 / RWKV_HEAD_DIM
    mu = _group_sum(y, bd_ref, split=True) * inv_n
    yc = y - mu
    var = _group_sum(yc * yc, bd_ref) * inv_n
    yn = yc * lax.rsqrt(var + RWKV_GN_EPS) * ln_w + ln_b
    bonus = _group_sum(r_ref[...] * k_ref[...] * r_k, bd_ref, split=True) * v_ref[...]
    o_ref[...] = x_ref[...] + _dot((yn + bonus) * g_ref[...], wo_ref[...])


def _rwkv_post(x, y, r, k, v, g, vecs, wo):
    n = x.shape[0]
    tm = min(256, n)
    assert n % tm == 0
    row = _row_spec(tm, D_MODEL)
    wo, wo_spec = _weight(wo)
    return pl.pallas_call(
        _rwkv_post_body,
        grid=(n // tm,),
        in_specs=[row] * 6 + [_const_spec(vecs.shape), _const_spec((MXU_DIM, MXU_DIM)), wo_spec],
        out_specs=row,
        out_shape=jax.ShapeDtypeStruct((n, D_MODEL), F32),
        compiler_params=_params(("parallel",)),
        name="rwkv_post",
    )(x, y, r, k, v, g, vecs, _block_diag_ones(RWKV_HEAD_DIM), wo)


def _attn_layer(x, pos, batch, seq, i, w, cache=None):
    lam_init = 0.8 - 0.6 * math.exp(-0.3 * i)
    q, k, kb, v, vt = _attn_proj(x, w["g"], w["wq"], w["wk"], w["wv"], w["qk_norm"], pos)
    if cache is None:
        o = _attn_prompt(q, kb, vt, w["lam"], w["subln"], lam_init, batch, seq)
    else:
        o = _attn_sample(q, k, v, *cache, w["lam"], w["subln"], lam_init, seq)
    shape = (batch, seq, ATT_HEADS, 2 * ATT_HEAD_DIM)
    return (o, w["wo"]), k.reshape(shape), v.reshape(shape)


def _ret_layer(x, pos, batch, seq, w, s0):
    q, k, v, sg = _ret_proj(x, w["g"], w["wq"], w["wk"], w["wv"], w["wg"], pos)
    og, s = _ret_core(q, k, v, sg, s0, batch, seq)
    return (og, w["wo"]), s


def _rwkv_layer(x, batch, seq, w, shift, s0):
    last = _norm(x.reshape(batch, seq, D_MODEL)[:, -1], w["g"])
    r, lw, k, v, na, bb, g = _rwkv_pre(x, w["g"], shift, seq, w["mix"], w["wr"], w["wk"], w["wv"], w["w1"], w["w2"],
                                       w["a1"], w["a2"], w["g1"], w["g2"], w["pre_vecs"])
    y, s = _rwkv_scan(r, lw, k, v, na, bb, s0, batch, seq)
    out = _rwkv_post(x, y, r, k, v, g, w["post_vecs"], w["wo"])
    return out, s, last


def kernel(x_prompt, x_sample, cache_k, cache_v, page_table, state_ret, state_rwkv, state_rwkv_shift, norm_g, ffn_w_gate, ffn_w_up, ffn_w_down, attn_w_q, attn_w_k, attn_w_v, attn_w_o, attn_qk_norm, attn_lambda, attn_subln, ret_w_q, ret_w_k, ret_w_v, ret_w_g, ret_w_o, rwkv_mix, rwkv_w_r, rwkv_w_k, rwkv_w_v, rwkv_w_o, rwkv_w0, rwkv_w1, rwkv_w2, rwkv_a0, rwkv_a1, rwkv_a2, rwkv_g1, rwkv_g2, rwkv_k_k, rwkv_k_a, rwkv_r_k, rwkv_ln_w, rwkv_ln_b):
    b_p, t_p, d = x_prompt.shape
    b_s, t_s, _ = x_sample.shape
    depth = norm_g.shape[0]
    past = page_table.shape[1] * cache_k.shape[2]
    pos_p = jnp.tile(jnp.arange(t_p, dtype=jnp.float32), b_p)
    pos_s = jnp.tile(past + jnp.arange(t_s, dtype=jnp.float32), b_s)
    yp = x_prompt.reshape(b_p * t_p, d)
    ys = x_sample.reshape(b_s * t_s, d)
    cast = lambda *arrs: [a.astype(BF16) for a in arrs]
    ffn_g, ffn_u, ffn_d = cast(ffn_w_gate, ffn_w_up, ffn_w_down)
    att_q, att_k, att_v, att_o = cast(attn_w_q, attn_w_k, attn_w_v, attn_w_o)
    ret_q, ret_k, ret_v, ret_g, ret_o = cast(ret_w_q, ret_w_k, ret_w_v, ret_w_g, ret_w_o)
    wkv_r, wkv_k, wkv_v, wkv_o = cast(rwkv_w_r, rwkv_w_k, rwkv_w_v, rwkv_w_o)
    wkv_w1, wkv_w2, wkv_a1, wkv_a2, wkv_g1, wkv_g2 = cast(rwkv_w1, rwkv_w2, rwkv_a1, rwkv_a2, rwkv_g1, rwkv_g2)
    outs = {name: [] for name in ("akp", "avp", "aks", "avs", "rsp", "rss", "wsp", "wss", "shp", "shs")}
    for i in range(depth):
        kind, j = i % 3, i // 3
        wg, wu, wd = [[(a, (i, half)) for half in range(2)] for a in (ffn_g, ffn_u, ffn_d)]
        yp = _ffn(yp, norm_g[i, 0], wg[0], wu[0], wd[0])
        ys = _ffn(ys, norm_g[i, 0], wg[0], wu[0], wd[0])
        if kind == 0:
            w = dict(g=norm_g[i, 1], wq=(att_q, (j,)), wk=(att_k, (j,)), wv=(att_v, (j,)), wo=(att_o, (j,)),
                     qk_norm=attn_qk_norm[j], lam=attn_lambda[j], subln=attn_subln[j])
            proj_p, kp, vp = _attn_layer(yp, pos_p, b_p, t_p, i, w)
            proj_s, ks, vs = _attn_layer(ys, pos_s, b_s, t_s, i, w, cache=(cache_k, cache_v, page_table, j))
            outs["akp"].append(kp)
            outs["avp"].append(vp)
            outs["aks"].append(ks)
            outs["avs"].append(vs)
        elif kind == 1:
            w = dict(g=norm_g[i, 1], wq=(ret_q, (j,)), wk=(ret_k, (j,)), wv=(ret_v, (j,)), wg=(ret_g, (j,)),
                     wo=(ret_o, (j,)))
            proj_p, sp = _ret_layer(yp, pos_p, b_p, t_p, w, None)
            proj_s, ss = _ret_layer(ys, pos_s, b_s, t_s, w, state_ret[j])
            outs["rsp"].append(sp)
            outs["rss"].append(ss)
        else:
            w = dict(g=norm_g[i, 1], mix=rwkv_mix[j], wr=(wkv_r, (j,)), wk=(wkv_k, (j,)), wv=(wkv_v, (j,)),
                     wo=(wkv_o, (j,)), w1=(wkv_w1, (j,)), w2=(wkv_w2, (j,)), a1=(wkv_a1, (j,)), a2=(wkv_a2, (j,)),
                     g1=(wkv_g1, (j,)), g2=(wkv_g2, (j,)),
                     pre_vecs=jnp.stack([rwkv_w0[j], rwkv_a0[j], rwkv_k_k[j], rwkv_k_a[j]]),
                     post_vecs=jnp.stack([rwkv_r_k[j].reshape(d), rwkv_ln_w[j], rwkv_ln_b[j]]))
            yp, sp, lp = _rwkv_layer(yp, b_p, t_p, w, jnp.zeros((b_p, d), F32), None)
            ys, ss, ls = _rwkv_layer(ys, b_s, t_s, w, state_rwkv_shift[j], state_rwkv[j])
            proj_p = proj_s = None
            outs["wsp"].append(sp)
            outs["wss"].append(ss)
            outs["shp"].append(lp)
            outs["shs"].append(ls)
        yp = _ffn(yp, norm_g[i, 2], wg[1], wu[1], wd[1], proj=proj_p)
        ys = _ffn(ys, norm_g[i, 2], wg[1], wu[1], wd[1], proj=proj_s)
    return (yp.reshape(b_p, t_p, d), ys.reshape(b_s, t_s, d)) + tuple(
        jnp.stack(outs[name]) for name in ("akp", "avp", "aks", "avs", "rsp", "rss", "wsp", "wss", "shp", "shs"))
```
